```python
import jax, jax.numpy as jnp
from jax import lax
import numpy as np

D_MODEL = 1024
BATCH = 2
SEQ = 16384
DEPTH = 1

R_HEADS = 8
R_HEAD_DIM = 64
R_WIDTH = R_HEADS * R_HEAD_DIM
DECAY_LORA = 64
ICLR_LORA = 64
GATE_LORA = 160
LNX_EPS = 64e-5
R_COLS = 3 * R_WIDTH + DECAY_LORA + ICLR_LORA + GATE_LORA
N_Q_HEADS = 8
N_KV_HEADS = 2
GQA = N_Q_HEADS // N_KV_HEADS
HEAD_DIM = 64
N_WIDTH = N_Q_HEADS * HEAD_DIM
KV_WIDTH = N_KV_HEADS * HEAD_DIM
N_COLS = N_WIDTH + 6 * KV_WIDTH + 3 * N_Q_HEADS
CMP_BLOCK = 32
CMP_STRIDE = 16
CMP_HIDDEN = 256
SEL_BLOCK = 64
SEL_TOPK = 16
WINDOW = 512
Q_BLOCK = 128
ROPE_THETA = 500000.0
ROT_DIM = HEAD_DIM // 4
IN_COLS = R_COLS + N_COLS
MIX_WIDTH = R_WIDTH + N_WIDTH
D_FF = ((8 * D_MODEL + 3 * 256 - 1) // (3 * 256)) * 256
NORM_EPS = 1e-6
MASK = -1e30
FORCE = 1e6

kernel_name = "hymba_rwkv7_nsa_hybrid_layer"


def rms_norm(x, g):
    xf = x.astype(jnp.float32)
    y = xf * lax.rsqrt(jnp.mean(xf * xf, axis=-1, keepdims=True) + NORM_EPS)
    return (y * g.astype(jnp.float32)).astype(x.dtype)


def partial_rope(x, pos):
    half = ROT_DIM // 2
    inv_freq = jnp.power(jnp.float32(ROPE_THETA), -jnp.arange(half, dtype=jnp.float32) * 2.0 / ROT_DIM)
    ang = pos.astype(jnp.float32)[..., None] * inv_freq
    cos = jnp.cos(ang)[:, :, None, :]
    sin = jnp.sin(ang)[:, :, None, :]
    xf = x.astype(jnp.float32)
    x1 = xf[..., :half]
    x2 = xf[..., half:ROT_DIM]
    out = jnp.concatenate([x1 * cos - x2 * sin, x2 * cos + x1 * sin, xf[..., ROT_DIM:]], axis=-1)
    return out.astype(x.dtype)


def rwkv7_group(cols, mu_shift, w0, w2, a0, a2, g2, k_k, k_a, r_k, lnx_w, lnx_b):
    B, S, _ = cols.shape
    f32 = jnp.float32
    prev = jnp.pad(cols, ((0, 0), (1, 0), (0, 0)))[:, :-1]
    cols = cols + (prev - cols) * mu_shift
    o = 0
    r = cols[..., o:o + R_WIDTH]; o += R_WIDTH
    k = cols[..., o:o + R_WIDTH]; o += R_WIDTH
    v = cols[..., o:o + R_WIDTH]; o += R_WIDTH
    w_lat = cols[..., o:o + DECAY_LORA]; o += DECAY_LORA
    a_lat = cols[..., o:o + ICLR_LORA]; o += ICLR_LORA
    g_lat = cols[..., o:o + GATE_LORA]

    w = -jax.nn.softplus(-(w0 + jnp.tanh(w_lat) @ w2).astype(f32)) - 0.5
    decay = jnp.exp(-jnp.exp(w))
    a = jax.nn.sigmoid((a0 + a_lat @ a2).astype(f32))
    g = jax.nn.sigmoid(g_lat) @ g2
    kk = (k * k_k).astype(f32)
    k_mod = k.astype(f32) * (1.0 + (a - 1.0) * k_a.astype(f32))

    heads = lambda t: t.astype(f32).reshape(B, S, R_HEADS, R_HEAD_DIM)
    kk = heads(kk)
    kk = kk / jnp.maximum(jnp.sqrt(jnp.sum(kk * kk, axis=-1, keepdims=True)), 1e-12)
    rh, kh, vh, ah, wh = heads(r), heads(k_mod), heads(v), heads(a), heads(decay)

    def step(state, inp):
        r_t, w_t, k_t, v_t, an_t, b_t = inp
        sa = jnp.einsum('bhvk,bhk->bhv', state, an_t)
        state = state * w_t[:, :, None, :] + sa[..., None] * b_t[:, :, None, :] + v_t[..., None] * k_t[:, :, None, :]
        return state, jnp.einsum('bhvk,bhk->bhv', state, r_t)

    seq_first = lambda t: jnp.moveaxis(t, 1, 0)
    state0 = jnp.zeros((B, R_HEADS, R_HEAD_DIM, R_HEAD_DIM), f32)
    xs = (seq_first(rh), seq_first(wh), seq_first(kh), seq_first(vh), seq_first(-kk), seq_first(kk * ah))
    _, y = lax.scan(step, state0, xs)
    y = jnp.moveaxis(y, 0, 1)

    mean = jnp.mean(y, axis=-1, keepdims=True)
    var = jnp.mean(jnp.square(y - mean), axis=-1, keepdims=True)
    y = ((y - mean) * lax.rsqrt(var + LNX_EPS)).reshape(B, S, R_WIDTH) * lnx_w + lnx_b
    bonus = jnp.sum(rh * kh * r_k.astype(f32), axis=-1, keepdims=True) * vh
    y = (y + bonus.reshape(B, S, R_WIDTH)) * g
    return y.astype(cols.dtype)


def compress(t, pe, w1, b1, w2):
    B, Hk, S, D = t.shape
    n_blk = S // CMP_STRIDE
    n_sub = CMP_BLOCK // CMP_STRIDE
    n_cmp = n_blk - n_sub + 1
    blocks = t.reshape(B, Hk, n_blk, CMP_STRIDE, D)
    win = jnp.concatenate([blocks[:, :, i:i + n_cmp] for i in range(n_sub)], axis=3)
    win = (win + pe).reshape(B, Hk, n_cmp, CMP_BLOCK * D)
    return jax.nn.silu(win @ w1 + b1) @ w2


def nsa_group(cols, positions, pe_k, wk1, bk1, wk2, pe_v, wv1, bv1, wv2):
    B, S, _ = cols.shape
    f32 = jnp.float32
    scale = HEAD_DIM ** -0.5
    q = partial_rope(cols[..., :N_WIDTH].reshape(B, S, N_Q_HEADS, HEAD_DIM), positions)
    q = q.reshape(B, S, N_KV_HEADS, GQA, HEAD_DIM).transpose(0, 2, 3, 1, 4)

    def kv(i, rope):
        o = N_WIDTH + i * KV_WIDTH
        t = cols[..., o:o + KV_WIDTH].reshape(B, S, N_KV_HEADS, HEAD_DIM)
        if rope:
            t = partial_rope(t, positions)
        return t.transpose(0, 2, 1, 3)

    k_cmp, v_cmp = kv(0, False), kv(1, False)
    k_slc, v_slc = kv(2, True), kv(3, True)
    k_win, v_win = kv(4, True), kv(5, True)
    gates = jax.nn.sigmoid(cols[..., N_WIDTH + 6 * KV_WIDTH:].reshape(B, S, N_KV_HEADS, GQA, 3))
    gates = gates.transpose(0, 2, 3, 1, 4)

    kc = compress(k_cmp, pe_k, wk1, bk1, wk2)
    vc = compress(v_cmp, pe_v, wv1, bv1, wv2)
    n_cmp = kc.shape[2]
    pos_c = positions[:, CMP_BLOCK - 1::CMP_STRIDE][:, :n_cmp]
    kc = partial_rope(kc.transpose(0, 2, 1, 3), pos_c).transpose(0, 2, 1, 3)
    cmp_end = jnp.arange(n_cmp) * CMP_STRIDE + CMP_BLOCK - 1

    n_sel = S // SEL_BLOCK
    n_top = min(SEL_TOPK, n_sel)
    ks_blocks = k_slc.reshape(B, N_KV_HEADS, n_sel, SEL_BLOCK, HEAD_DIM)
    vs_blocks = v_slc.reshape(B, N_KV_HEADS, n_sel, SEL_BLOCK, HEAD_DIM)
    c_start = jnp.arange(n_cmp)[:, None] * CMP_STRIDE
    s_start = jnp.arange(n_sel)[None, :] * SEL_BLOCK
    cover = ((c_start < s_start + SEL_BLOCK) & (c_start + CMP_BLOCK > s_start)).astype(f32)
    sel_ids = jnp.arange(n_sel)
    bidx = jnp.arange(B)[:, None, None, None]
    hidx = jnp.arange(N_KV_HEADS)[None, :, None, None]

    kw_pad = jnp.pad(k_win, ((0, 0), (0, 0), (WINDOW, 0), (0, 0)))
    vw_pad = jnp.pad(v_win, ((0, 0), (0, 0), (WINDOW, 0), (0, 0)))

    def q_block(qb):
        start = qb * Q_BLOCK
        t = start + jnp.arange(Q_BLOCK)
        qq = lax.dynamic_slice_in_dim(q, start, Q_BLOCK, axis=3)
        gg = lax.dynamic_slice_in_dim(gates, start, Q_BLOCK, axis=3)

        s = jnp.einsum('bkgqd,bknd->bkgqn', qq, kc).astype(f32) * scale
        valid = cmp_end[None, :] <= t[:, None]
        p = jax.nn.softmax(jnp.where(valid, s, MASK), axis=-1)
        p = jnp.where(valid, p, 0.0)
        o_cmp = jnp.einsum('bkgqn,bknd->bkgqd', p.astype(vc.dtype), vc)

        imp = jnp.sum(p, axis=2) @ cover
        cur = t // SEL_BLOCK
        forced = (sel_ids[None] == 0) | (sel_ids[None] == cur[:, None]) | (sel_ids[None] == cur[:, None] - 1)
        score = jnp.where(forced, FORCE, jnp.where(sel_ids[None] <= cur[:, None], imp, MASK))
        _, idx = lax.top_k(score, n_top)
        k_sel = ks_blocks[bidx, hidx, idx]
        v_sel = vs_blocks[bidx, hidx, idx]
        tok = idx[..., None] * SEL_BLOCK + jnp.arange(SEL_BLOCK)
        valid = (tok <= t[:, None, None])[:, :, None]
        s = jnp.einsum('bkgqd,bkqjld->bkgqjl', qq, k_sel).astype(f32) * scale
        s = jnp.where(valid, s, MASK).reshape(B, N_KV_HEADS, GQA, Q_BLOCK, n_top * SEL_BLOCK)
        p = jax.nn.softmax(s, axis=-1).reshape(B, N_KV_HEADS, GQA, Q_BLOCK, n_top, SEL_BLOCK)
        o_sel = jnp.einsum('bkgqjl,bkqjld->bkgqd', p.astype(v_sel.dtype), v_sel)

        kw = lax.dynamic_slice_in_dim(kw_pad, start, WINDOW + Q_BLOCK, axis=2)
        vw = lax.dynamic_slice_in_dim(vw_pad, start, WINDOW + Q_BLOCK, axis=2)
        kpos = start - WINDOW + jnp.arange(WINDOW + Q_BLOCK)
        valid = (kpos[None] <= t[:, None]) & (kpos[None] > t[:, None] - WINDOW) & (kpos[None] >= 0)
        s = jnp.einsum('bkgqd,bksd->bkgqs', qq, kw).astype(f32) * scale
        p = jax.nn.softmax(jnp.where(valid, s, MASK), axis=-1)
        o_win = jnp.einsum('bkgqs,bksd->bkgqd', p.astype(vw.dtype), vw)

        o = gg[..., 0:1] * o_cmp + gg[..., 1:2] * o_sel + gg[..., 2:3] * o_win
        return o.astype(cols.dtype)

    out = lax.map(q_block, jnp.arange(S // Q_BLOCK))
    out = jnp.moveaxis(out, 0, 3).reshape(B, N_KV_HEADS, GQA, S, HEAD_DIM)
    return out.transpose(0, 3, 1, 2, 4).reshape(B, S, N_WIDTH)


def swiglu(x, w_gate, w_up, w_down):
    return (jax.nn.silu(x @ w_gate) * (x @ w_up)) @ w_down


def setup_inputs(seed: int = 0) -> dict:
    key = jax.random.key(seed)
    ks = iter(jax.random.split(key, 40))
    f32 = jnp.float32
    L = DEPTH
    nrm = lambda shape, s: jax.random.normal(next(ks), shape, f32) * s
    uni = lambda shape, lo, hi: jax.random.uniform(next(ks), shape, f32, lo, hi)
    x = nrm((BATCH, SEQ, D_MODEL), 1.0)
    positions = (jax.random.randint(next(ks), (BATCH, 1), 0, 4096, jnp.int32)
                 + jnp.arange(SEQ, dtype=jnp.int32)[None, :])
    return {
        "x": x,
        "positions": positions,
        "norm_mix": 1.0 + nrm((L, D_MODEL), 0.02),
        "w_in": nrm((L, D_MODEL, IN_COLS), D_MODEL ** -0.5),
        "mu_shift": uni((L, R_COLS), 0.0, 1.0),
        "w0": uni((L, R_WIDTH), -6.0, -1.0),
        "w2": nrm((L, DECAY_LORA, R_WIDTH), 0.1),
        "a0": nrm((L, R_WIDTH), 0.1),
        "a2": nrm((L, ICLR_LORA, R_WIDTH), 0.1),
        "g2": nrm((L, GATE_LORA, R_WIDTH), GATE_LORA ** -0.5),
        "k_k": 0.85 + nrm((L, R_WIDTH), 0.05),
        "k_a": 1.0 + nrm((L, R_WIDTH), 0.05),
        "r_k": nrm((L, R_HEADS, R_HEAD_DIM), 0.1),
        "lnx_w": 1.0 + nrm((L, R_WIDTH), 0.02),
        "lnx_b": nrm((L, R_WIDTH), 0.01),
        "pe_k": nrm((L, CMP_BLOCK, HEAD_DIM), 0.1),
        "wk1": nrm((L, CMP_BLOCK * HEAD_DIM, CMP_HIDDEN), (CMP_BLOCK * HEAD_DIM) ** -0.5),
        "bk1": nrm((L, CMP_HIDDEN), 0.01),
        "wk2": nrm((L, CMP_HIDDEN, HEAD_DIM), CMP_HIDDEN ** -0.5),
        "pe_v": nrm((L, CMP_BLOCK, HEAD_DIM), 0.1),
        "wv1": nrm((L, CMP_BLOCK * HEAD_DIM, CMP_HIDDEN), (CMP_BLOCK * HEAD_DIM) ** -0.5),
        "bv1": nrm((L, CMP_HIDDEN), 0.01),
        "wv2": nrm((L, CMP_HIDDEN, HEAD_DIM), CMP_HIDDEN ** -0.5),
        "w_out": nrm((L, MIX_WIDTH, D_MODEL), MIX_WIDTH ** -0.5),
        "norm_ffn": 1.0 + nrm((L, D_MODEL), 0.02),
        "w_gate": nrm((L, D_MODEL, D_FF), D_MODEL ** -0.5),
        "w_up": nrm((L, D_MODEL, D_FF), D_MODEL ** -0.5),
        "w_down": nrm((L, D_FF, D_MODEL), D_FF ** -0.5),
        "norm_final": 1.0 + nrm((D_MODEL,), 0.02),
    }


def reference(x, positions, norm_mix, w_in, mu_shift, w0, w2, a0, a2, g2, k_k, k_a, r_k, lnx_w, lnx_b,
              pe_k, wk1, bk1, wk2, pe_v, wv1, bv1, wv2, w_out, norm_ffn, w_gate, w_up, w_down, norm_final):
    h = x
    for l in range(DEPTH):
        xn = rms_norm(h, norm_mix[l])
        proj = xn @ w_in[l]
        y_r = rwkv7_group(proj[..., :R_COLS], mu_shift[l], w0[l], w2[l], a0[l], a2[l], g2[l],
                          k_k[l], k_a[l], r_k[l], lnx_w[l], lnx_b[l])
        y_n = nsa_group(proj[..., R_COLS:], positions, pe_k[l], wk1[l], bk1[l], wk2[l],
                        pe_v[l], wv1[l], bv1[l], wv2[l])
        h = h + jnp.concatenate([y_r, y_n], axis=-1) @ w_out[l]
        h = h + swiglu(rms_norm(h, norm_ffn[l]), w_gate[l], w_up[l], w_down[l])
    return rms_norm(h, norm_final)
```

```python
import functools

import jax
import jax.numpy as jnp
from jax import lax
from jax.experimental import pallas as pl
from jax.experimental.pallas import tpu as pltpu

F32 = jnp.float32
MXU_DTYPE = jnp.bfloat16

R_HEADS = 8
HEAD = 64
R_WIDTH = R_HEADS * HEAD
DECAY_LORA = 64
ICLR_LORA = 64
GATE_LORA = 160
LNX_EPS = 64e-5
R_COLS = 3 * R_WIDTH + DECAY_LORA + ICLR_LORA + GATE_LORA
N_Q_HEADS = 8
N_KV_HEADS = 2
GQA = N_Q_HEADS // N_KV_HEADS
N_WIDTH = N_Q_HEADS * HEAD
KV_WIDTH = N_KV_HEADS * HEAD
CMP_BLOCK = 32
CMP_STRIDE = 16
CMP_HIDDEN = 256
SEL_BLOCK = 64
SEL_TOPK = 16
WINDOW = 512
Q_BLOCK = 128
ROPE_THETA = 500000.0
ROT_DIM = HEAD // 4
ROT_HALF = ROT_DIM // 2
NORM_EPS = 1e-6
MASK = -1e30
FORCE = 1e6
REMOVED = -3e38

LANES = 128
R_COLS_PAD = 1920
GLAT_PAD = 256
N_COLS_PAD = N_WIDTH + 6 * KV_WIDTH + LANES
GATE_ROWS = 32
CHUNK = 64
VMEM_LIMIT = 56 * 1024 * 1024


def _dot(a, b):
    return jnp.dot(a, b, preferred_element_type=F32)


def _dot_nt(a, b):
    return lax.dot_general(a, b, (((1,), (1,)), ((), ())), preferred_element_type=F32)


def _mx(a):
    return a.astype(MXU_DTYPE)


def _rms(x, w):
    ms = jnp.mean(x * x, axis=-1, keepdims=True)
    return x * lax.rsqrt(ms + NORM_EPS) * w


def _rope_tables(pos_col, freq_row, sgn_row):
    ang = pos_col.astype(F32) * freq_row
    return jnp.cos(ang), jnp.sin(ang) * sgn_row


def _rope128(t, cos, sin, first):
    sh = jnp.where(first, pltpu.roll(t, LANES - ROT_HALF, 1), pltpu.roll(t, ROT_HALF, 1))
    return t * cos + sh * sin


def _proj_kernel(x_ref, pos_ref, nw_ref, freq_ref, sgn_ref, wr_ref, wn_ref,
                 rw_ref, q_ref, kvc_ref, ks_ref, vst_ref, kw_ref, vwt_ref, gt_ref):
    tm = x_ref.shape[0]
    xn = _mx(_rms(x_ref[...], nw_ref[...]))
    rw_ref[...] = _dot(xn, wr_ref[...])
    pn = _dot(xn, wn_ref[...])
    cos, sin = _rope_tables(pos_ref[...], freq_ref[...], sgn_ref[...])
    lane = lax.broadcasted_iota(jnp.int32, (1, LANES), 1)
    first = (lane % HEAD) < ROT_HALF
    rope = lambda t: _rope128(t, cos, sin, first)
    scale = HEAD ** -0.5
    q = [rope(pn[:, i * LANES:(i + 1) * LANES]) * scale for i in range(N_WIDTH // LANES)]
    q_ref[...] = jnp.concatenate(q, axis=1).astype(q_ref.dtype)
    o = N_WIDTH
    kvc_ref[...] = pn[:, o:o + 2 * KV_WIDTH]
    o += 2 * KV_WIDTH
    for k_ref, vt_ref in ((ks_ref, vst_ref), (kw_ref, vwt_ref)):
        kk = rope(pn[:, o:o + KV_WIDTH])
        vt = rope(pn[:, o + KV_WIDTH:o + 2 * KV_WIDTH]).T
        o += 2 * KV_WIDTH
        for h in range(N_KV_HEADS):
            k_ref[h] = kk[:, h * HEAD:(h + 1) * HEAD].astype(k_ref.dtype)
            for i in range(tm // LANES):
                vt_ref[h, i] = vt[h * HEAD:(h + 1) * HEAD, i * LANES:(i + 1) * LANES].astype(vt_ref.dtype)
    gt_ref[...] = jax.nn.sigmoid(pn[:, o:o + LANES]).T[:GATE_ROWS]


def _proj(x2, pos_col, nw, freq, sgn, w_r, w_n, B, S, tm):
    BS, D = x2.shape
    nt = S // tm
    nkt = S // LANES
    row = lambda b, i: (b * nt + i, 0)
    const = lambda b, i: (0, 0)
    kv_spec = pl.BlockSpec((None, N_KV_HEADS, tm, HEAD), lambda b, i: (b, 0, i, 0))
    vt_spec = pl.BlockSpec((None, N_KV_HEADS, tm // LANES, HEAD, LANES), lambda b, i: (b, 0, i, 0, 0))
    kv_shape = jax.ShapeDtypeStruct((B, N_KV_HEADS, S, HEAD), MXU_DTYPE)
    vt_shape = jax.ShapeDtypeStruct((B, N_KV_HEADS, nkt, HEAD, LANES), MXU_DTYPE)
    return pl.pallas_call(
        _proj_kernel,
        grid=(B, nt),
        in_specs=[
            pl.BlockSpec((tm, D), row),
            pl.BlockSpec((tm, 1), row),
            pl.BlockSpec((1, D), const),
            pl.BlockSpec((1, LANES), const),
            pl.BlockSpec((1, LANES), const),
            pl.BlockSpec((D, R_COLS_PAD), const),
            pl.BlockSpec((D, N_COLS_PAD), const),
        ],
        out_specs=[
            pl.BlockSpec((tm, R_COLS_PAD), row),
            pl.BlockSpec((tm, N_WIDTH), row),
            pl.BlockSpec((tm, 2 * KV_WIDTH), row),
            kv_spec, vt_spec, kv_spec, vt_spec,
            pl.BlockSpec((None, GATE_ROWS, tm), lambda b, i: (b, 0, i)),
        ],
        out_shape=[
            jax.ShapeDtypeStruct((BS, R_COLS_PAD), F32),
            jax.ShapeDtypeStruct((BS, N_WIDTH), MXU_DTYPE),
            jax.ShapeDtypeStruct((BS, 2 * KV_WIDTH), F32),
            kv_shape, vt_shape, kv_shape, vt_shape,
            jax.ShapeDtypeStruct((B, GATE_ROWS, S), F32),
        ],
        compiler_params=pltpu.CompilerParams(
            dimension_semantics=("parallel", "parallel"), vmem_limit_bytes=VMEM_LIMIT),
        name="proj",
    )(x2, pos_col, nw, freq, sgn, w_r, w_n)


def _rwkv_kernel(rw_ref, mu_ref, w0_ref, w2_ref, a0_ref, a2_ref, g2_ref, kk_ref, ka_ref, rk_ref,
                 lw_ref, lb_ref, gmat_ref, ltri_ref, y_ref,
                 state_s, prev_s, at_s, rt_s, bt_s, kt_s, v_s, wc_s, y_s, bonus_s, g_s):
    T = rw_ref.shape[0]

    @pl.when(pl.program_id(1) == 0)
    def _():
        state_s[...] = jnp.zeros_like(state_s)
        prev_s[...] = jnp.zeros_like(prev_s)

    x = rw_ref[...]
    row = lax.broadcasted_iota(jnp.int32, (T, 1), 0)
    xs = jnp.where(row == 0, prev_s[0:1, :], pltpu.roll(x, 1, 0))
    prev_s[0:1, :] = x[T - 1:T, :]
    xm = x + (xs - x) * mu_ref[...]
    r = xm[:, 0:R_WIDTH]
    k = xm[:, R_WIDTH:2 * R_WIDTH]
    v = xm[:, 2 * R_WIDTH:3 * R_WIDTH]
    lat = xm[:, 3 * R_WIDTH:3 * R_WIDTH + LANES]
    glat = xm[:, 3 * R_WIDTH + LANES:3 * R_WIDTH + LANES + GLAT_PAD]

    gmat = gmat_ref[...]

    def gsum(t):
        hi = _mx(t)
        lo = _mx(t - hi.astype(F32))
        return _dot(hi, gmat) + _dot(lo, gmat)

    wv = w0_ref[...] + _dot(_mx(jnp.tanh(lat)), w2_ref[...])
    w = -(jnp.maximum(-wv, 0.0) + jnp.log(1.0 + jnp.exp(-jnp.abs(wv)))) - 0.5
    ld = -jnp.exp(w)
    a = jax.nn.sigmoid(a0_ref[...] + _dot(_mx(lat), a2_ref[...]))
    g_s[...] = _dot(_mx(jax.nn.sigmoid(glat)), g2_ref[...])
    kkv = k * kk_ref[...]
    kkn = kkv / jnp.maximum(jnp.sqrt(gsum(kkv * kkv)), 1e-12)
    k_mod = k * (1.0 + (a - 1.0) * ka_ref[...])
    bonus_s[...] = gsum(r * k_mod * rk_ref[...]) * v

    ltri = ltri_ref[...]
    h1 = _mx(ld)
    r1 = ld - h1.astype(F32)
    h2 = _mx(r1)
    h3 = _mx(r1 - h2.astype(F32))
    cum = _dot(ltri, h1) + _dot(ltri, h2) + _dot(ltri, h3)
    w_inc = jnp.exp(cum)
    w_inv = jnp.exp(-cum)
    w_exc = jnp.exp(cum - ld)
    at_s[...] = (-kkn * w_exc).astype(at_s.dtype)
    rt_s[...] = (r * w_inc).astype(rt_s.dtype)
    bt_s[...] = (kkn * a * w_inv).astype(bt_s.dtype)
    kt_s[...] = (k_mod * w_inv).astype(kt_s.dtype)
    v_s[...] = v
    wc_s[...] = w_inc

    ii = lax.broadcasted_iota(jnp.int32, (CHUNK, CHUNK), 0)
    jj = lax.broadcasted_iota(jnp.int32, (CHUNK, CHUNK), 1)
    strict = ii > jj
    incl = ii >= jj
    eye = (ii == jj).astype(F32)

    def chunk_body(c, carry):
        r0 = pl.multiple_of(c * CHUNK, CHUNK)
        rows = pl.ds(r0, CHUNK)
        for h in range(R_HEADS):
            sl = slice(h * HEAD, (h + 1) * HEAD)
            a_ = at_s[rows, sl]
            r_ = rt_s[rows, sl]
            b_ = bt_s[rows, sl]
            k_ = kt_s[rows, sl]
            vf = v_s[rows, sl]
            v_ = _mx(vf)
            ar = jnp.concatenate([a_, r_], axis=0)
            bk = jnp.concatenate([b_, k_], axis=0)
            mb = _dot_nt(ar, b_)
            mk = _dot_nt(ar, k_)
            a_ab = jnp.where(strict, mb[:CHUNK], 0.0)
            a_rb = jnp.where(incl, mb[CHUNK:], 0.0)
            a_ak = jnp.where(strict, mk[:CHUNK], 0.0)
            a_rk = jnp.where(incl, mk[CHUNK:], 0.0)
            xp = _dot(_mx(a_ab), _mx(a_ab))
            qs = eye + a_ab
            n = 2
            while n < CHUNK:
                xb = _mx(xp)
                if 2 * n < CHUNK:
                    res = _dot(_mx(jnp.concatenate([xp, qs], axis=0)), xb)
                    xp = res[:CHUNK]
                    qs = qs + res[CHUNK:]
                else:
                    qs = qs + _dot(_mx(qs), xb)
                n *= 2
            s0 = state_s[h]
            asrs = _dot_nt(ar, _mx(s0))
            rhs = asrs[:CHUNK] + _dot(_mx(a_ak), v_)
            u = _dot(_mx(qs), _mx(rhs))
            y = asrs[CHUNK:] + _dot(_mx(a_rb), _mx(u)) + _dot(_mx(a_rk), v_)
            uv_t = jnp.concatenate([u, vf], axis=0).T
            ds = _dot(_mx(uv_t), bk)
            wc = wc_s[pl.ds(r0 + CHUNK - 8, 8), sl][7:8]
            state_s[h] = (s0 + ds) * wc
            y_s[rows, sl] = y
        return carry

    lax.fori_loop(0, T // CHUNK, chunk_body, 0)

    y = y_s[...]
    mean = gsum(y) * (1.0 / HEAD)
    d = y - mean
    var = gsum(d * d) * (1.0 / HEAD)
    yn = d * lax.rsqrt(var + LNX_EPS) * lw_ref[...] + lb_ref[...]
    y_ref[...] = ((yn + bonus_s[...]) * g_s[...]).astype(y_ref.dtype)


def _rwkv(rw, mu, w0, w2p, a0, a2p, g2p, k_k, k_a, r_k, lnx_w, lnx_b, gmat, ltri, B, S, T):
    BS = rw.shape[0]
    nt = S // T
    const = lambda b, i: (0, 0)
    vec = pl.BlockSpec((1, R_WIDTH), const)
    big = lambda dt: pltpu.VMEM((T, R_WIDTH), dt)
    return pl.pallas_call(
        _rwkv_kernel,
        grid=(B, nt),
        in_specs=[
            pl.BlockSpec((T, R_COLS_PAD), lambda b, i: (b * nt + i, 0)),
            pl.BlockSpec((1, R_COLS_PAD), const),
            vec,
            pl.BlockSpec((LANES, R_WIDTH), const),
            vec,
            pl.BlockSpec((LANES, R_WIDTH), const),
            pl.BlockSpec((GLAT_PAD, R_WIDTH), const),
            vec, vec, vec, vec, vec,
            pl.BlockSpec((R_WIDTH, R_WIDTH), const),
            pl.BlockSpec((T, T), const),
        ],
        out_specs=pl.BlockSpec((T, R_WIDTH), lambda b, i: (b * nt + i, 0)),
        out_shape=jax.ShapeDtypeStruct((BS, R_WIDTH), MXU_DTYPE),
        scratch_shapes=[
            pltpu.VMEM((R_HEADS, HEAD, HEAD), F32),
            pltpu.VMEM((8, R_COLS_PAD), F32),
            big(MXU_DTYPE), big(MXU_DTYPE), big(MXU_DTYPE), big(MXU_DTYPE),
            big(F32), big(F32), big(F32), big(F32), big(F32),
        ],
        compiler_params=pltpu.CompilerParams(
            dimension_semantics=("parallel", "arbitrary"), vmem_limit_bytes=VMEM_LIMIT),
        name="rwkv",
    )(rw, mu, w0, w2p, a0, a2p, g2p, k_k, k_a, r_k, lnx_w, lnx_b, gmat, ltri)


def _cmp_kernel(x_ref, w1_ref, b1_ref, w2_ref, pe_ref, pos_ref, freq_ref, sgn_ref, o_ref, ot_ref):
    n_blk = x_ref.shape[0]
    half = x_ref.shape[1]
    j = pl.program_id(1)
    xb = _mx(x_ref[...])
    w1 = w1_ref[...]
    first_half = _dot(xb, w1[:half])
    second_half = pltpu.roll(_dot(xb, w1[half:]), n_blk - 1, 0)
    bias = _dot(_mx(pe_ref[...]), w1)[0:1] + b1_ref[...]
    hdn = first_half + second_half + bias
    hdn = hdn * jax.nn.sigmoid(hdn)
    out = _dot(_mx(hdn), w2_ref[...])
    cos, sin = _rope_tables(pos_ref[...], freq_ref[...], sgn_ref[...])
    lane = lax.broadcasted_iota(jnp.int32, (1, LANES), 1)
    roped = _rope128(out, cos, sin, (lane % HEAD) < ROT_HALF)
    res = jnp.where(j < N_KV_HEADS, roped, out)
    row = lax.broadcasted_iota(jnp.int32, (n_blk, 1), 0)
    res = jnp.where(row == n_blk - 1, 0.0, res)
    o_ref[...] = res[:, :HEAD].astype(o_ref.dtype)
    ot_ref[...] = res.T[:HEAD].astype(ot_ref.dtype)


def _compress(xblk, w1s, b1s, w2s, pes, pos_c, freq, sgn, B, n_blk):
    half = xblk.shape[-1]
    return pl.pallas_call(
        _cmp_kernel,
        grid=(B, 2 * N_KV_HEADS),
        in_specs=[
            pl.BlockSpec((None, None, n_blk, half), lambda b, j: (b, j, 0, 0)),
            pl.BlockSpec((None, 2 * half, CMP_HIDDEN), lambda b, j: (j // N_KV_HEADS, 0, 0)),
            pl.BlockSpec((None, 1, CMP_HIDDEN), lambda b, j: (j // N_KV_HEADS, 0, 0)),
            pl.BlockSpec((None, CMP_HIDDEN, LANES), lambda b, j: (j // N_KV_HEADS, 0, 0)),
            pl.BlockSpec((None, 8, 2 * half), lambda b, j: (j // N_KV_HEADS, 0, 0)),
            pl.BlockSpec((None, n_blk, 1), lambda b, j: (b, 0, 0)),
            pl.BlockSpec((1, LANES), lambda b, j: (0, 0)),
            pl.BlockSpec((1, LANES), lambda b, j: (0, 0)),
        ],
        out_specs=[
            pl.BlockSpec((None, None, n_blk, HEAD), lambda b, j: (b, j, 0, 0)),
            pl.BlockSpec((None, None, HEAD, n_blk), lambda b, j: (b, j, 0, 0)),
        ],
        out_shape=[
            jax.ShapeDtypeStruct((B, 2 * N_KV_HEADS, n_blk, HEAD), MXU_DTYPE),
            jax.ShapeDtypeStruct((B, 2 * N_KV_HEADS, HEAD, n_blk), MXU_DTYPE),
        ],
        compiler_params=pltpu.CompilerParams(
            dimension_semantics=("parallel", "parallel"), vmem_limit_bytes=VMEM_LIMIT),
        name="compress",
    )(xblk, w1s, b1s, w2s, pes, pos_c, freq, sgn)


def _nsa_kernel(q_ref, kc_ref, vct_ref, ks_ref, vst_ref, kw_ref, vwt_ref, g_ref, o_ref, sel_s,
                *, n_top):
    n_blk = kc_ref.shape[0]
    n_sel = sel_s.shape[0]
    hk = pl.program_id(1)
    qb = pl.program_id(2)
    start = qb * Q_BLOCK
    W = GQA * Q_BLOCK

    q_t = q_ref[...].astype(F32).T
    qt = _mx(jnp.concatenate([q_t[g * HEAD:(g + 1) * HEAD] for g in range(GQA)], axis=1))
    tile4 = lambda t: jnp.concatenate([t] * GQA, axis=1)
    qi = lax.broadcasted_iota(jnp.int32, (1, Q_BLOCK), 1)
    t_q = start + qi
    t_w = tile4(t_q)

    sc = _dot(kc_ref[...], qt)
    c_end = lax.broadcasted_iota(jnp.int32, (n_blk, 1), 0) * CMP_STRIDE + (CMP_BLOCK - 1)
    valid = c_end <= t_w
    sc = jnp.where(valid, sc, MASK)
    m = jnp.max(sc, axis=0, keepdims=True)
    p = jnp.where(valid, jnp.exp(sc - m), 0.0)
    l = jnp.sum(p, axis=0, keepdims=True)
    pn = p * (1.0 / jnp.where(l > 0.0, l, 1.0))
    o_cmp = _dot(vct_ref[...], _mx(pn))

    ps = pn[:, 0:Q_BLOCK]
    for g in range(1, GQA):
        ps = ps + pn[:, g * Q_BLOCK:(g + 1) * Q_BLOCK]
    hi = _mx(ps)
    lo = _mx(ps - hi.astype(F32))
    j_col = lax.broadcasted_iota(jnp.int32, (n_sel, 1), 0)
    c_row = lax.broadcasted_iota(jnp.int32, (1, n_blk), 1)
    cover = ((c_row * CMP_STRIDE < j_col * SEL_BLOCK + SEL_BLOCK)
             & (c_row * CMP_STRIDE + CMP_BLOCK > j_col * SEL_BLOCK))
    cover = jnp.where(cover, 1.0, 0.0).astype(MXU_DTYPE)
    imp = _dot(cover, hi) + _dot(cover, lo)

    cur = t_q // SEL_BLOCK
    forced = (j_col == 0) | (j_col == cur) | (j_col == cur - 1)
    score = jnp.where(forced, FORCE, jnp.where(j_col <= cur, imp, MASK))
    sel = jnp.zeros((n_sel, Q_BLOCK), F32)
    for _ in range(n_top):
        mx = jnp.max(score, axis=0, keepdims=True)
        idx = jnp.min(jnp.where(score == mx, j_col, n_sel), axis=0, keepdims=True)
        pick = j_col == idx
        sel = jnp.where(pick, 1.0, sel)
        score = jnp.where(pick, REMOVED, score)
    sel_s[...] = sel

    n_row = lax.broadcasted_iota(jnp.int32, (Q_BLOCK, 1), 0)
    row8 = lax.broadcasted_iota(jnp.int32, (8, 1), 0)

    def attend(k_tile, vt_tile, ok, carry):
        m_, l_, acc = carry
        s = _dot(k_tile, qt) + tile4(jnp.where(ok, 0.0, MASK))
        m_new = jnp.maximum(m_, jnp.max(s, axis=0, keepdims=True))
        alpha = jnp.exp(m_ - m_new)
        pr = jnp.exp(s - m_new)
        l_new = alpha * l_ + jnp.sum(pr, axis=0, keepdims=True)
        acc_new = alpha * acc + _dot(vt_tile, _mx(pr))
        return m_new, l_new, acc_new

    init = (jnp.full((1, W), MASK, F32), jnp.zeros((1, W), F32), jnp.zeros((HEAD, W), F32))

    def sel_body(jt, carry):
        r0 = pl.multiple_of(jt * Q_BLOCK, Q_BLOCK)
        grp = sel_s[pl.ds(pl.multiple_of((jt // 4) * 8, 8), 8), :]
        r_lo = (2 * jt) % 8
        s_lo = jnp.sum(jnp.where(row8 == r_lo, grp, 0.0), axis=0, keepdims=True)
        s_hi = jnp.sum(jnp.where(row8 == r_lo + 1, grp, 0.0), axis=0, keepdims=True)
        chosen = jnp.where(n_row < SEL_BLOCK, s_lo, s_hi) > 0.5
        ok = chosen & ((r0 + n_row) <= t_q)
        return attend(ks_ref[pl.ds(r0, Q_BLOCK), :], vst_ref[jt], ok, carry)

    _, l_sel, acc_sel = lax.fori_loop(0, qb + 1, sel_body, init)
    o_sel = acc_sel * (1.0 / l_sel)

    def win_body(i, carry):
        jt = qb - i
        jc = jnp.maximum(jt, 0)
        r0 = pl.multiple_of(jc * Q_BLOCK, Q_BLOCK)
        tok = r0 + n_row
        ok = (tok <= t_q) & (tok > t_q - WINDOW) & (jt >= 0)
        return attend(kw_ref[pl.ds(r0, Q_BLOCK), :], vwt_ref[jc], ok, carry)

    _, l_win, acc_win = lax.fori_loop(0, WINDOW // Q_BLOCK + 1, win_body, init)
    o_win = acc_win * (1.0 / l_win)

    outs = []
    gates = g_ref[...]
    per_kv = GQA * 3
    gate = lambda r: jnp.where(hk == 0, gates[r:r + 1], gates[per_kv + r:per_kv + r + 1])
    for g in range(GQA):
        cols = slice(g * Q_BLOCK, (g + 1) * Q_BLOCK)
        o_g = (gate(g * 3) * o_cmp[:, cols] + gate(g * 3 + 1) * o_sel[:, cols]
               + gate(g * 3 + 2) * o_win[:, cols])
        outs.append(o_g)
    o_ref[...] = jnp.concatenate(outs, axis=0).T.astype(o_ref.dtype)


def _nsa(q, kc, vct, ks, vst, kw, vwt, gt, B, S):
    BS = q.shape[0]
    nqb = S // Q_BLOCK
    n_blk = S // CMP_STRIDE
    n_sel = S // SEL_BLOCK
    n_top = min(SEL_TOPK, n_sel)
    per_head = lambda *blk: pl.BlockSpec((None, None) + blk, lambda b, h, i: (b, h) + (0,) * len(blk))
    qspec = pl.BlockSpec((Q_BLOCK, GQA * HEAD), lambda b, h, i: (b * nqb + i, h))
    return pl.pallas_call(
        functools.partial(_nsa_kernel, n_top=n_top),
        grid=(B, N_KV_HEADS, nqb),
        in_specs=[
            qspec,
            per_head(n_blk, HEAD),
            per_head(HEAD, n_blk),
            per_head(S, HEAD),
            per_head(nqb, HEAD, LANES),
            per_head(S, HEAD),
            per_head(nqb, HEAD, LANES),
            pl.BlockSpec((None, GATE_ROWS, Q_BLOCK), lambda b, h, i: (b, 0, i)),
        ],
        out_specs=qspec,
        out_shape=jax.ShapeDtypeStruct((BS, N_WIDTH), MXU_DTYPE),
        scratch_shapes=[pltpu.VMEM((n_sel, Q_BLOCK), F32)],
        compiler_params=pltpu.CompilerParams(
            dimension_semantics=("parallel", "parallel", "arbitrary"), vmem_limit_bytes=VMEM_LIMIT),
        name="nsa",
    )(q, kc, vct, ks, vst, kw, vwt, gt)


def _ffn_kernel(x_ref, yr_ref, yn_ref, wo_ref, nf_ref, wg_ref, wu_ref, wd_ref, nfin_ref, o_ref,
                h1_s, xn_s, acc_s):
    f = pl.program_id(1)

    @pl.when(f == 0)
    def _():
        mix = _dot(yr_ref[...], wo_ref[:R_WIDTH]) + _dot(yn_ref[...], wo_ref[R_WIDTH:])
        h1 = x_ref[...] + mix
        h1_s[...] = h1
        xn_s[...] = _rms(h1, nf_ref[...]).astype(xn_s.dtype)
        acc_s[...] = jnp.zeros_like(acc_s)

    xn = xn_s[...]
    gate = _dot(xn, wg_ref[...])
    up = _dot(xn, wu_ref[...])
    acc_s[...] += _dot(_mx(gate * jax.nn.sigmoid(gate) * up), wd_ref[...])

    @pl.when(f == pl.num_programs(1) - 1)
    def _():
        o_ref[...] = _rms(h1_s[...] + acc_s[...], nfin_ref[...])


def _ffn(x2, y_r, y_n, w_out, nf, w_gate, w_up, w_down, nfin, tm, tf):
    BS, D = x2.shape
    d_ff = w_gate.shape[1]
    row = lambda i, f: (i, 0)
    const = lambda i, f: (0, 0)
    return pl.pallas_call(
        _ffn_kernel,
        grid=(BS // tm, d_ff // tf),
        in_specs=[
            pl.BlockSpec((tm, D), row),
            pl.BlockSpec((tm, R_WIDTH), row),
            pl.BlockSpec((tm, N_WIDTH), row),
            pl.BlockSpec((R_WIDTH + N_WIDTH, D), const),
            pl.BlockSpec((1, D), const),
            pl.BlockSpec((D, tf), lambda i, f: (0, f)),
            pl.BlockSpec((D, tf), lambda i, f: (0, f)),
            pl.BlockSpec((tf, D), lambda i, f: (f, 0)),
            pl.BlockSpec((1, D), const),
        ],
        out_specs=pl.BlockSpec((tm, D), row),
        out_shape=jax.ShapeDtypeStruct((BS, D), F32),
        scratch_shapes=[pltpu.VMEM((tm, D), F32), pltpu.VMEM((tm, D), MXU_DTYPE), pltpu.VMEM((tm, D), F32)],
        compiler_params=pltpu.CompilerParams(
            dimension_semantics=("parallel", "arbitrary"), vmem_limit_bytes=VMEM_LIMIT),
        name="ffn",
    )(x2, y_r, y_n, w_out, nf, w_gate, w_up, w_down, nfin)


def _pad_to(a, n, axis):
    pad = [(0, 0)] * a.ndim
    pad[axis] = (0, n - a.shape[axis])
    return jnp.pad(a, pad)


def _rope_rows():
    lane = jnp.arange(LANES) % HEAD
    inv_freq = jnp.power(jnp.float32(ROPE_THETA), -jnp.arange(ROT_HALF, dtype=F32) * 2.0 / ROT_DIM)
    freq = jnp.where(lane < ROT_DIM, inv_freq[lane % ROT_HALF], 0.0).astype(F32)[None, :]
    sgn = jnp.where(lane < ROT_HALF, -1.0, jnp.where(lane < ROT_DIM, 1.0, 0.0)).astype(F32)[None, :]
    return freq, sgn


def _layer(h, positions, norm_mix, w_in, mu_shift, w0, w2, a0, a2, g2, k_k, k_a, r_k, lnx_w, lnx_b,
           pe_k, wk1, bk1, wk2, pe_v, wv1, bv1, wv2, w_out, norm_ffn, w_gate, w_up, w_down, norm_final):
    B, S, D = h.shape
    BS = B * S
    x2 = h.reshape(BS, D)
    freq, sgn = _rope_rows()
    row = lambda a: a.reshape(1, -1).astype(F32)

    w_r = _mx(_pad_to(w_in[:, :R_COLS], R_COLS_PAD, 1))
    w_n = _mx(_pad_to(w_in[:, R_COLS:], N_COLS_PAD, 1))
    tm = min(256, S)
    rw, q, kvc, ks, vst, kw, vwt, gt = _proj(
        x2, positions.reshape(BS, 1), row(norm_mix), freq, sgn, w_r, w_n, B, S, tm)

    T = min(256, S)
    head_of = jnp.arange(R_WIDTH) // HEAD
    gmat = (head_of[:, None] == head_of[None, :]).astype(MXU_DTYPE)
    tt = jnp.arange(T)
    ltri = ((tt[:, None] // CHUNK == tt[None, :] // CHUNK) & (tt[:, None] >= tt[None, :])).astype(MXU_DTYPE)
    w2p = _mx(_pad_to(w2, LANES, 0))
    a2p = _mx(jnp.concatenate([jnp.zeros((DECAY_LORA, R_WIDTH), F32), a2], axis=0))
    g2p = _mx(_pad_to(g2, GLAT_PAD, 0))
    y_r = _rwkv(rw, row(_pad_to(mu_shift, R_COLS_PAD, 0)), row(w0), w2p, row(a0), a2p, g2p,
                row(k_k), row(k_a), row(r_k), row(lnx_w), row(lnx_b), gmat, ltri, B, S, T)

    n_blk = S // CMP_STRIDE
    xblk = kvc.reshape(B, n_blk, CMP_STRIDE, 2 * N_KV_HEADS, HEAD).transpose(0, 3, 1, 2, 4)
    xblk = xblk.reshape(B, 2 * N_KV_HEADS, n_blk, CMP_STRIDE * HEAD)
    w1s = _mx(jnp.stack([wk1, wv1]))
    b1s = jnp.stack([bk1, bv1]).reshape(2, 1, CMP_HIDDEN).astype(F32)
    w2s = _mx(_pad_to(jnp.stack([wk2, wv2]), LANES, 2))
    pes = _pad_to(jnp.stack([pe_k, pe_v]).reshape(2, 1, CMP_BLOCK * HEAD), 8, 1).astype(F32)
    pos_c = _pad_to(positions[:, CMP_BLOCK - 1::CMP_STRIDE], n_blk, 1).reshape(B, n_blk, 1)
    cmp_rows, cmp_t = _compress(xblk, w1s, b1s, w2s, pes, pos_c, freq, sgn, B, n_blk)
    kc = cmp_rows[:, :N_KV_HEADS]
    vct = cmp_t[:, N_KV_HEADS:]

    y_n = _nsa(q, kc, vct, ks, vst, kw, vwt, gt, B, S)

    d_ff = w_gate.shape[1]
    tf = d_ff // 2 if (d_ff // 2) % LANES == 0 else d_ff
    out = _ffn(x2, y_r, y_n, _mx(w_out), row(norm_ffn), _mx(w_gate), _mx(w_up), _mx(w_down),
               row(norm_final), min(512, BS), tf)
    return out.reshape(B, S, D)


def kernel(x, positions, norm_mix, w_in, mu_shift, w0, w2, a0, a2, g2, k_k, k_a, r_k, lnx_w, lnx_b, pe_k, wk1, bk1, wk2, pe_v, wv1, bv1, wv2, w_out, norm_ffn, w_gate, w_up, w_down, norm_final):
    assert norm_mix.shape[0] == 1, "single-layer block"
    l = 0
    return _layer(x, positions, norm_mix[l], w_in[l], mu_shift[l], w0[l], w2[l], a0[l], a2[l], g2[l],
                  k_k[l], k_a[l], r_k[l].reshape(-1), lnx_w[l], lnx_b[l], pe_k[l], wk1[l], bk1[l], wk2[l],
                  pe_v[l], wv1[l], bv1[l], wv2[l], w_out[l], norm_ffn[l], w_gate[l], w_up[l], w_down[l],
                  norm_final)
```

```python
import functools

import jax
import jax.numpy as jnp
from jax import lax
from jax.experimental import pallas as pl
from jax.experimental.pallas import tpu as pltpu

F32 = jnp.float32
MXU_DTYPE = jnp.bfloat16

R_HEADS = 8
HEAD = 64
R_WIDTH = R_HEADS * HEAD
DECAY_LORA = 64
ICLR_LORA = 64
GATE_LORA = 160
LNX_EPS = 64e-5
R_COLS = 3 * R_WIDTH + DECAY_LORA + ICLR_LORA + GATE_LORA
N_Q_HEADS = 8
N_KV_HEADS = 2
GQA = N_Q_HEADS // N_KV_HEADS
N_WIDTH = N_Q_HEADS * HEAD
KV_WIDTH = N_KV_HEADS * HEAD
CMP_BLOCK = 32
CMP_STRIDE = 16
CMP_HIDDEN = 256
SEL_BLOCK = 64
SEL_TOPK = 16
WINDOW = 512
Q_BLOCK = 128
ROPE_THETA = 500000.0
ROT_DIM = HEAD // 4
ROT_HALF = ROT_DIM // 2
NORM_EPS = 1e-6
MASK = -1e30
FORCE = 1e6
REMOVED = -3e38

LANES = 128
R_COLS_PAD = 1920
GLAT_PAD = 256
N_COLS_PAD = N_WIDTH + 6 * KV_WIDTH + LANES
GATE_ROWS = 32
CHUNK = 64
SEL_GROUP = 8
GROUP = SEL_GROUP * SEL_BLOCK
KV_TILE = 256
ONES_ROWS = 16
LOG2E = 1.4426950408889634
VMEM_LIMIT = 56 * 1024 * 1024


def _dot(a, b):
    return jnp.dot(a, b, preferred_element_type=F32)


def _dot_nt(a, b):
    return lax.dot_general(a, b, (((1,), (1,)), ((), ())), preferred_element_type=F32)


def _mx(a):
    return a.astype(MXU_DTYPE)


def _rms(x, w):
    ms = jnp.mean(x * x, axis=-1, keepdims=True)
    return x * lax.rsqrt(ms + NORM_EPS) * w


def _rope_tables(pos_col, freq_row, sgn_row):
    ang = pos_col.astype(F32) * freq_row
    return jnp.cos(ang), jnp.sin(ang) * sgn_row


def _rope128(t, cos, sin, first):
    sh = jnp.where(first, pltpu.roll(t, LANES - ROT_HALF, 1), pltpu.roll(t, ROT_HALF, 1))
    return t * cos + sh * sin


def _proj_kernel(x_ref, pos_ref, nw_ref, freq_ref, sgn_ref, wr_ref, wn_ref,
                 rw_ref, q_ref, kvc_ref, ks_ref, vst_ref, kw_ref, vwt_ref, gt_ref):
    tm = x_ref.shape[0]
    xn = _mx(_rms(x_ref[...], nw_ref[...]))
    rw_ref[...] = _dot(xn, wr_ref[...])
    pn = _dot(xn, wn_ref[...])
    cos, sin = _rope_tables(pos_ref[...], freq_ref[...], sgn_ref[...])
    lane = lax.broadcasted_iota(jnp.int32, (1, LANES), 1)
    first = (lane % HEAD) < ROT_HALF
    rope = lambda t: _rope128(t, cos, sin, first)
    scale = HEAD ** -0.5 * LOG2E
    q = [rope(pn[:, i * LANES:(i + 1) * LANES]) * scale for i in range(N_WIDTH // LANES)]
    q_ref[...] = jnp.concatenate(q, axis=1).astype(q_ref.dtype)
    o = N_WIDTH
    kvc_ref[...] = pn[:, o:o + 2 * KV_WIDTH]
    o += 2 * KV_WIDTH

    row_g = pl.program_id(1) * tm + lax.broadcasted_iota(jnp.int32, (tm, 1), 0)
    onehot = jnp.where(lane == HEAD + (row_g // SEL_BLOCK) % SEL_GROUP, 1.0, 0.0)
    kk = rope(pn[:, o:o + KV_WIDTH])
    vt = rope(pn[:, o + KV_WIDTH:o + 2 * KV_WIDTH]).T
    o += 2 * KV_WIDTH
    ones_rows = jnp.where(lax.broadcasted_iota(jnp.int32, (ONES_ROWS, KV_TILE), 0) == 0, 1.0, 0.0)
    for h in range(N_KV_HEADS):
        k_h = kk if h == 0 else pltpu.roll(kk, HEAD, 1)
        ks_ref[h] = jnp.where(lane < HEAD, k_h, onehot).astype(ks_ref.dtype)
        for i in range(tm // KV_TILE):
            v_t = vt[h * HEAD:(h + 1) * HEAD, i * KV_TILE:(i + 1) * KV_TILE]
            vst_ref[h, i] = jnp.concatenate([v_t, ones_rows], axis=0).astype(vst_ref.dtype)

    kk = rope(pn[:, o:o + KV_WIDTH])
    vt = rope(pn[:, o + KV_WIDTH:o + 2 * KV_WIDTH]).T
    o += 2 * KV_WIDTH
    for h in range(N_KV_HEADS):
        kw_ref[h] = kk[:, h * HEAD:(h + 1) * HEAD].astype(kw_ref.dtype)
        for i in range(tm // LANES):
            vwt_ref[h, i] = vt[h * HEAD:(h + 1) * HEAD, i * LANES:(i + 1) * LANES].astype(vwt_ref.dtype)
    gt_ref[...] = jax.nn.sigmoid(pn[:, o:o + LANES]).T[:GATE_ROWS]


def _proj(x2, pos_col, nw, freq, sgn, w_r, w_n, B, S, tm):
    BS, D = x2.shape
    nt = S // tm
    nkt = S // LANES
    row = lambda b, i: (b * nt + i, 0)
    const = lambda b, i: (0, 0)
    head_rows = lambda width: pl.BlockSpec((None, N_KV_HEADS, tm, width), lambda b, i: (b, 0, i, 0))
    head_tiles = lambda rows, width: pl.BlockSpec(
        (None, N_KV_HEADS, tm // width, rows, width), lambda b, i: (b, 0, i, 0, 0))
    kv = lambda *shape: jax.ShapeDtypeStruct((B, N_KV_HEADS) + shape, MXU_DTYPE)
    return pl.pallas_call(
        _proj_kernel,
        grid=(B, nt),
        in_specs=[
            pl.BlockSpec((tm, D), row),
            pl.BlockSpec((tm, 1), row),
            pl.BlockSpec((1, D), const),
            pl.BlockSpec((1, LANES), const),
            pl.BlockSpec((1, LANES), const),
            pl.BlockSpec((D, R_COLS_PAD), const),
            pl.BlockSpec((D, N_COLS_PAD), const),
        ],
        out_specs=[
            pl.BlockSpec((tm, R_COLS_PAD), row),
            pl.BlockSpec((tm, N_WIDTH), row),
            pl.BlockSpec((tm, 2 * KV_WIDTH), row),
            head_rows(LANES), head_tiles(HEAD + ONES_ROWS, KV_TILE),
            head_rows(HEAD), head_tiles(HEAD, LANES),
            pl.BlockSpec((None, GATE_ROWS, tm), lambda b, i: (b, 0, i)),
        ],
        out_shape=[
            jax.ShapeDtypeStruct((BS, R_COLS_PAD), F32),
            jax.ShapeDtypeStruct((BS, N_WIDTH), MXU_DTYPE),
            jax.ShapeDtypeStruct((BS, 2 * KV_WIDTH), F32),
            kv(S, LANES), kv(S // KV_TILE, HEAD + ONES_ROWS, KV_TILE),
            kv(S, HEAD), kv(nkt, HEAD, LANES),
            jax.ShapeDtypeStruct((B, GATE_ROWS, S), F32),
        ],
        compiler_params=pltpu.CompilerParams(
            dimension_semantics=("parallel", "parallel"), vmem_limit_bytes=VMEM_LIMIT),
        name="proj",
    )(x2, pos_col, nw, freq, sgn, w_r, w_n)


def _rwkv_kernel(rw_ref, mu_ref, w0_ref, w2_ref, a0_ref, a2_ref, g2_ref, kk_ref, ka_ref, rk_ref,
                 lw_ref, lb_ref, gmat_ref, ltri_ref, y_ref,
                 state_s, prev_s, at_s, rt_s, bt_s, kt_s, v_s, wc_s, y_s, bonus_s, g_s):
    T = rw_ref.shape[0]

    @pl.when(pl.program_id(1) == 0)
    def _():
        state_s[...] = jnp.zeros_like(state_s)
        prev_s[...] = jnp.zeros_like(prev_s)

    x = rw_ref[...]
    row = lax.broadcasted_iota(jnp.int32, (T, 1), 0)
    xs = jnp.where(row == 0, prev_s[0:1, :], pltpu.roll(x, 1, 0))
    prev_s[0:1, :] = x[T - 1:T, :]
    xm = x + (xs - x) * mu_ref[...]
    r = xm[:, 0:R_WIDTH]
    k = xm[:, R_WIDTH:2 * R_WIDTH]
    v = xm[:, 2 * R_WIDTH:3 * R_WIDTH]
    lat = xm[:, 3 * R_WIDTH:3 * R_WIDTH + LANES]
    glat = xm[:, 3 * R_WIDTH + LANES:3 * R_WIDTH + LANES + GLAT_PAD]

    gmat = gmat_ref[...]

    def gsum(t):
        hi = _mx(t)
        lo = _mx(t - hi.astype(F32))
        return _dot(hi, gmat) + _dot(lo, gmat)

    wv = w0_ref[...] + _dot(_mx(jnp.tanh(lat)), w2_ref[...])
    w = -(jnp.maximum(-wv, 0.0) + jnp.log(1.0 + jnp.exp(-jnp.abs(wv)))) - 0.5
    ld = -jnp.exp(w)
    a = jax.nn.sigmoid(a0_ref[...] + _dot(_mx(lat), a2_ref[...]))
    g_s[...] = _dot(_mx(jax.nn.sigmoid(glat)), g2_ref[...])
    kkv = k * kk_ref[...]
    kkn = kkv / jnp.maximum(jnp.sqrt(gsum(kkv * kkv)), 1e-12)
    k_mod = k * (1.0 + (a - 1.0) * ka_ref[...])
    bonus_s[...] = gsum(r * k_mod * rk_ref[...]) * v

    ltri = ltri_ref[...]
    h1 = _mx(ld)
    r1 = ld - h1.astype(F32)
    h2 = _mx(r1)
    h3 = _mx(r1 - h2.astype(F32))
    cum = _dot(ltri, h1) + _dot(ltri, h2) + _dot(ltri, h3)
    w_inc = jnp.exp(cum)
    w_inv = jnp.exp(-cum)
    w_exc = jnp.exp(cum - ld)
    at_s[...] = (-kkn * w_exc).astype(at_s.dtype)
    rt_s[...] = (r * w_inc).astype(rt_s.dtype)
    bt_s[...] = (kkn * a * w_inv).astype(bt_s.dtype)
    kt_s[...] = (k_mod * w_inv).astype(kt_s.dtype)
    v_s[...] = v
    wc_s[...] = w_inc

    ii = lax.broadcasted_iota(jnp.int32, (CHUNK, CHUNK), 0)
    jj = lax.broadcasted_iota(jnp.int32, (CHUNK, CHUNK), 1)
    strict = ii > jj
    incl = ii >= jj
    eye = (ii == jj).astype(F32)

    def chunk_body(c, carry):
        r0 = pl.multiple_of(c * CHUNK, CHUNK)
        rows = pl.ds(r0, CHUNK)
        hs = range(R_HEADS)
        sl = [slice(h * HEAD, (h + 1) * HEAD) for h in hs]
        a_ = [at_s[rows, sl[h]] for h in hs]
        r_ = [rt_s[rows, sl[h]] for h in hs]
        b_ = [bt_s[rows, sl[h]] for h in hs]
        k_ = [kt_s[rows, sl[h]] for h in hs]
        vf = [v_s[rows, sl[h]] for h in hs]
        v_ = [_mx(t) for t in vf]
        ar = [jnp.concatenate([a_[h], r_[h]], axis=0) for h in hs]
        bk = [jnp.concatenate([b_[h], k_[h]], axis=0) for h in hs]
        mb = [_dot_nt(ar[h], b_[h]) for h in hs]
        mk = [_dot_nt(ar[h], k_[h]) for h in hs]
        s0 = [state_s[h] for h in hs]
        asrs = [_dot_nt(ar[h], _mx(s0[h])) for h in hs]
        a_ab = [jnp.where(strict, mb[h][:CHUNK], 0.0) for h in hs]
        a_rb = [_mx(jnp.where(incl, mb[h][CHUNK:], 0.0)) for h in hs]
        a_ak = [_mx(jnp.where(strict, mk[h][:CHUNK], 0.0)) for h in hs]
        a_rk = [_mx(jnp.where(incl, mk[h][CHUNK:], 0.0)) for h in hs]
        ab = [_mx(t) for t in a_ab]
        xp = [_dot(ab[h], ab[h]) for h in hs]
        qs = [eye + a_ab[h] for h in hs]
        akv = [_dot(a_ak[h], v_[h]) for h in hs]
        yk = [_dot(a_rk[h], v_[h]) for h in hs]
        n = 2
        while n < CHUNK:
            xb = [_mx(t) for t in xp]
            if 2 * n < CHUNK:
                res = [_dot(_mx(jnp.concatenate([xp[h], qs[h]], axis=0)), xb[h]) for h in hs]
                xp = [res[h][:CHUNK] for h in hs]
                qs = [qs[h] + res[h][CHUNK:] for h in hs]
            else:
                upd = [_dot(_mx(qs[h]), xb[h]) for h in hs]
                qs = [qs[h] + upd[h] for h in hs]
            n *= 2
        rhs = [_mx(asrs[h][:CHUNK] + akv[h]) for h in hs]
        u = [_dot(_mx(qs[h]), rhs[h]) for h in hs]
        yb = [_dot(a_rb[h], _mx(u[h])) for h in hs]
        uv_t = [_mx(jnp.concatenate([u[h], vf[h]], axis=0).T) for h in hs]
        ds = [_dot(uv_t[h], bk[h]) for h in hs]
        for h in hs:
            wc = wc_s[pl.ds(r0 + CHUNK - 8, 8), sl[h]][7:8]
            state_s[h] = (s0[h] + ds[h]) * wc
            y_s[rows, sl[h]] = asrs[h][CHUNK:] + yb[h] + yk[h]
        return carry

    lax.fori_loop(0, T // CHUNK, chunk_body, 0)

    y = y_s[...]
    mean = gsum(y) * (1.0 / HEAD)
    d = y - mean
    var = gsum(d * d) * (1.0 / HEAD)
    yn = d * lax.rsqrt(var + LNX_EPS) * lw_ref[...] + lb_ref[...]
    y_ref[...] = ((yn + bonus_s[...]) * g_s[...]).astype(y_ref.dtype)


def _rwkv(rw, mu, w0, w2p, a0, a2p, g2p, k_k, k_a, r_k, lnx_w, lnx_b, gmat, ltri, B, S, T):
    BS = rw.shape[0]
    nt = S // T
    const = lambda b, i: (0, 0)
    vec = pl.BlockSpec((1, R_WIDTH), const)
    big = lambda dt: pltpu.VMEM((T, R_WIDTH), dt)
    return pl.pallas_call(
        _rwkv_kernel,
        grid=(B, nt),
        in_specs=[
            pl.BlockSpec((T, R_COLS_PAD), lambda b, i: (b * nt + i, 0)),
            pl.BlockSpec((1, R_COLS_PAD), const),
            vec,
            pl.BlockSpec((LANES, R_WIDTH), const),
            vec,
            pl.BlockSpec((LANES, R_WIDTH), const),
            pl.BlockSpec((GLAT_PAD, R_WIDTH), const),
            vec, vec, vec, vec, vec,
            pl.BlockSpec((R_WIDTH, R_WIDTH), const),
            pl.BlockSpec((T, T), const),
        ],
        out_specs=pl.BlockSpec((T, R_WIDTH), lambda b, i: (b * nt + i, 0)),
        out_shape=jax.ShapeDtypeStruct((BS, R_WIDTH), MXU_DTYPE),
        scratch_shapes=[
            pltpu.VMEM((R_HEADS, HEAD, HEAD), F32),
            pltpu.VMEM((8, R_COLS_PAD), F32),
            big(MXU_DTYPE), big(MXU_DTYPE), big(MXU_DTYPE), big(MXU_DTYPE),
            big(F32), big(F32), big(F32), big(F32), big(F32),
        ],
        compiler_params=pltpu.CompilerParams(
            dimension_semantics=("parallel", "arbitrary"), vmem_limit_bytes=VMEM_LIMIT),
        name="rwkv",
    )(rw, mu, w0, w2p, a0, a2p, g2p, k_k, k_a, r_k, lnx_w, lnx_b, gmat, ltri)


def _cmp_kernel(x_ref, w1_ref, b1_ref, w2_ref, pe_ref, pos_ref, freq_ref, sgn_ref, o_ref, ot_ref):
    n_blk = x_ref.shape[0]
    half = x_ref.shape[1]
    j = pl.program_id(1)
    xb = _mx(x_ref[...])
    w1 = w1_ref[...]
    first_half = _dot(xb, w1[:half])
    second_half = pltpu.roll(_dot(xb, w1[half:]), n_blk - 1, 0)
    bias = _dot(_mx(pe_ref[...]), w1)[0:1] + b1_ref[...]
    hdn = first_half + second_half + bias
    hdn = hdn * jax.nn.sigmoid(hdn)
    out = _dot(_mx(hdn), w2_ref[...])
    cos, sin = _rope_tables(pos_ref[...], freq_ref[...], sgn_ref[...])
    lane = lax.broadcasted_iota(jnp.int32, (1, LANES), 1)
    roped = _rope128(out, cos, sin, (lane % HEAD) < ROT_HALF)
    res = jnp.where(j < N_KV_HEADS, roped, out)
    row = lax.broadcasted_iota(jnp.int32, (n_blk, 1), 0)
    res = jnp.where(row == n_blk - 1, 0.0, res)
    o_ref[...] = res[:, :HEAD].astype(o_ref.dtype)
    ot_ref[...] = res.T[:HEAD].astype(ot_ref.dtype)


def _compress(xblk, w1s, b1s, w2s, pes, pos_c, freq, sgn, B, n_blk):
    half = xblk.shape[-1]
    return pl.pallas_call(
        _cmp_kernel,
        grid=(B, 2 * N_KV_HEADS),
        in_specs=[
            pl.BlockSpec((None, None, n_blk, half), lambda b, j: (b, j, 0, 0)),
            pl.BlockSpec((None, 2 * half, CMP_HIDDEN), lambda b, j: (j // N_KV_HEADS, 0, 0)),
            pl.BlockSpec((None, 1, CMP_HIDDEN), lambda b, j: (j // N_KV_HEADS, 0, 0)),
            pl.BlockSpec((None, CMP_HIDDEN, LANES), lambda b, j: (j // N_KV_HEADS, 0, 0)),
            pl.BlockSpec((None, 8, 2 * half), lambda b, j: (j // N_KV_HEADS, 0, 0)),
            pl.BlockSpec((None, n_blk, 1), lambda b, j: (b, 0, 0)),
            pl.BlockSpec((1, LANES), lambda b, j: (0, 0)),
            pl.BlockSpec((1, LANES), lambda b, j: (0, 0)),
        ],
        out_specs=[
            pl.BlockSpec((None, None, n_blk, HEAD), lambda b, j: (b, j, 0, 0)),
            pl.BlockSpec((None, None, HEAD, n_blk), lambda b, j: (b, j, 0, 0)),
        ],
        out_shape=[
            jax.ShapeDtypeStruct((B, 2 * N_KV_HEADS, n_blk, HEAD), MXU_DTYPE),
            jax.ShapeDtypeStruct((B, 2 * N_KV_HEADS, HEAD, n_blk), MXU_DTYPE),
        ],
        compiler_params=pltpu.CompilerParams(
            dimension_semantics=("parallel", "parallel"), vmem_limit_bytes=VMEM_LIMIT),
        name="compress",
    )(xblk, w1s, b1s, w2s, pes, pos_c, freq, sgn)


def _nsa_kernel(q_ref, kc_ref, vct_ref, ks_ref, vst_ref, kw_ref, vwt_ref, g_ref, o_ref, sel_s,
                *, n_top):
    n_blk = kc_ref.shape[0]
    n_sel = sel_s.shape[0]
    hk = pl.program_id(1)
    qb = pl.program_id(2)
    start = qb * Q_BLOCK
    W = GQA * Q_BLOCK

    q_t = q_ref[...].astype(F32).T
    qt = _mx(jnp.concatenate([q_t[g * HEAD:(g + 1) * HEAD] for g in range(GQA)], axis=1))
    tile4 = lambda t: jnp.concatenate([t] * GQA, axis=1)
    qi = lax.broadcasted_iota(jnp.int32, (1, Q_BLOCK), 1)
    t_q = start + qi
    t_w = tile4(t_q)

    sc = _dot(kc_ref[...], qt)
    c_end = lax.broadcasted_iota(jnp.int32, (n_blk, 1), 0) * CMP_STRIDE + (CMP_BLOCK - 1)
    valid = c_end <= t_w
    sc = jnp.where(valid, sc, MASK)
    m = jnp.max(sc, axis=0, keepdims=True)
    p = jnp.where(valid, jnp.exp2(sc - m), 0.0)
    l = jnp.sum(p, axis=0, keepdims=True)
    pn = p * (1.0 / jnp.where(l > 0.0, l, 1.0))
    o_cmp = _dot(vct_ref[...], _mx(pn))

    ps = pn[:, 0:Q_BLOCK]
    for g in range(1, GQA):
        ps = ps + pn[:, g * Q_BLOCK:(g + 1) * Q_BLOCK]
    hi = _mx(ps)
    lo = _mx(ps - hi.astype(F32))
    j_col = lax.broadcasted_iota(jnp.int32, (n_sel, 1), 0)
    c_row = lax.broadcasted_iota(jnp.int32, (1, n_blk), 1)
    cover = ((c_row * CMP_STRIDE < j_col * SEL_BLOCK + SEL_BLOCK)
             & (c_row * CMP_STRIDE + CMP_BLOCK > j_col * SEL_BLOCK))
    cover = jnp.where(cover, 1.0, 0.0).astype(MXU_DTYPE)
    imp = _dot(cover, hi) + _dot(cover, lo)

    cur = t_q // SEL_BLOCK
    forced = (j_col == 0) | (j_col == cur) | (j_col == cur - 1)
    score = jnp.where(forced, FORCE, jnp.where(j_col <= cur, imp, MASK))
    sel = jnp.zeros((n_sel, Q_BLOCK), F32)
    for _ in range(n_top):
        mx = jnp.max(score, axis=0, keepdims=True)
        idx = jnp.min(jnp.where(score == mx, j_col, n_sel), axis=0, keepdims=True)
        pick = j_col == idx
        sel = jnp.where(pick, 1.0, sel)
        score = jnp.where(pick, REMOVED, score)
    sel_s[...] = sel

    tiles = GROUP // KV_TILE
    zero_rows = jnp.zeros((LANES - HEAD - 2 * SEL_GROUP, W), MXU_DTYPE)

    def group_scores(j):
        grp = sel_s[pl.ds(pl.multiple_of(j * SEL_GROUP, SEL_GROUP), SEL_GROUP), :]
        neg = jnp.where(grp > 0.5, 0.0, MASK)
        neg = tile4(jnp.concatenate([neg, jnp.zeros_like(neg)], axis=0))
        q_aug = jnp.concatenate([qt, _mx(neg), zero_rows], axis=0)
        r0 = pl.multiple_of(j * GROUP, GROUP)
        return [_dot(ks_ref[pl.ds(r0 + i * KV_TILE, KV_TILE), :], q_aug) for i in range(tiles)]

    def online(s, vt_tile, carry):
        m_, acc = carry
        m_new = jnp.maximum(m_, jnp.max(s, axis=0, keepdims=True))
        alpha = jnp.exp2(m_ - m_new)
        return m_new, alpha * acc + _dot(vt_tile, _mx(jnp.exp2(s - m_new)))

    def sel_body(j, carry):
        for i, s in enumerate(group_scores(j)):
            carry = online(s, vst_ref[j * tiles + i], carry)
        return carry

    init = (jnp.full((1, W), MASK, F32), jnp.zeros((HEAD + ONES_ROWS, W), F32))
    last = qb // (GROUP // Q_BLOCK)
    carry = lax.fori_loop(0, last, sel_body, init)

    s_diag = group_scores(last)
    w0 = pl.multiple_of(jnp.maximum(start - WINDOW, 0), Q_BLOCK)
    n_win = WINDOW + Q_BLOCK
    s_win = _dot(kw_ref[pl.ds(w0, n_win), :], qt)

    k_row = lax.broadcasted_iota(jnp.int32, (KV_TILE, 1), 0)
    for i, s in enumerate(s_diag):
        tok = last * GROUP + i * KV_TILE + k_row
        causal = tile4(jnp.where(tok <= t_q, 0.0, MASK))
        carry = online(s + causal, vst_ref[last * tiles + i], carry)
    _, acc_sel = carry
    o_sel = acc_sel[:HEAD] * (1.0 / acc_sel[HEAD:HEAD + 1])

    tok = w0 + lax.broadcasted_iota(jnp.int32, (n_win, 1), 0)
    in_win = (tok <= t_q) & (tok > t_q - WINDOW)
    s_win = s_win + tile4(jnp.where(in_win, 0.0, MASK))
    p_win = jnp.exp2(s_win - jnp.max(s_win, axis=0, keepdims=True))
    l_win = jnp.sum(p_win, axis=0, keepdims=True)
    p_win = _mx(p_win)
    acc_win = jnp.zeros((HEAD, W), F32)
    for i in range(n_win // Q_BLOCK):
        acc_win = acc_win + _dot(vwt_ref[w0 // Q_BLOCK + i], p_win[i * Q_BLOCK:(i + 1) * Q_BLOCK])
    o_win = acc_win * (1.0 / l_win)

    outs = []
    gates = g_ref[...]
    per_kv = GQA * 3
    gate = lambda r: jnp.where(hk == 0, gates[r:r + 1], gates[per_kv + r:per_kv + r + 1])
    for g in range(GQA):
        cols = slice(g * Q_BLOCK, (g + 1) * Q_BLOCK)
        o_g = (gate(g * 3) * o_cmp[:, cols] + gate(g * 3 + 1) * o_sel[:, cols]
               + gate(g * 3 + 2) * o_win[:, cols])
        outs.append(o_g)
    o_ref[...] = jnp.concatenate(outs, axis=0).T.astype(o_ref.dtype)


def _nsa(q, kc, vct, ks, vst, kw, vwt, gt, B, S):
    BS = q.shape[0]
    nqb = S // Q_BLOCK
    n_blk = S // CMP_STRIDE
    n_sel = S // SEL_BLOCK
    n_top = min(SEL_TOPK, n_sel)
    per_head = lambda *blk: pl.BlockSpec((None, None) + blk, lambda b, h, i: (b, h) + (0,) * len(blk))
    qspec = pl.BlockSpec((Q_BLOCK, GQA * HEAD), lambda b, h, i: (b * nqb + i, h))
    return pl.pallas_call(
        functools.partial(_nsa_kernel, n_top=n_top),
        grid=(B, N_KV_HEADS, nqb),
        in_specs=[
            qspec,
            per_head(n_blk, HEAD),
            per_head(HEAD, n_blk),
            per_head(S, LANES),
            per_head(S // KV_TILE, HEAD + ONES_ROWS, KV_TILE),
            per_head(S, HEAD),
            per_head(nqb, HEAD, LANES),
            pl.BlockSpec((None, GATE_ROWS, Q_BLOCK), lambda b, h, i: (b, 0, i)),
        ],
        out_specs=qspec,
        out_shape=jax.ShapeDtypeStruct((BS, N_WIDTH), MXU_DTYPE),
        scratch_shapes=[pltpu.VMEM((n_sel, Q_BLOCK), F32)],
        compiler_params=pltpu.CompilerParams(
            dimension_semantics=("parallel", "parallel", "arbitrary"), vmem_limit_bytes=VMEM_LIMIT),
        name="nsa",
    )(q, kc, vct, ks, vst, kw, vwt, gt)


def _ffn_kernel(x_ref, yr_ref, yn_ref, wo_ref, nf_ref, wg_ref, wu_ref, wd_ref, nfin_ref, o_ref,
                h1_s, xn_s, acc_s):
    f = pl.program_id(1)

    @pl.when(f == 0)
    def _():
        mix = _dot(yr_ref[...], wo_ref[:R_WIDTH]) + _dot(yn_ref[...], wo_ref[R_WIDTH:])
        h1 = x_ref[...] + mix
        h1_s[...] = h1
        xn_s[...] = _rms(h1, nf_ref[...]).astype(xn_s.dtype)
        acc_s[...] = jnp.zeros_like(acc_s)

    xn = xn_s[...]
    gate = _dot(xn, wg_ref[...])
    up = _dot(xn, wu_ref[...])
    acc_s[...] += _dot(_mx(gate * jax.nn.sigmoid(gate) * up), wd_ref[...])

    @pl.when(f == pl.num_programs(1) - 1)
    def _():
        o_ref[...] = _rms(h1_s[...] + acc_s[...], nfin_ref[...])


def _ffn(x2, y_r, y_n, w_out, nf, w_gate, w_up, w_down, nfin, tm, tf):
    BS, D = x2.shape
    d_ff = w_gate.shape[1]
    row = lambda i, f: (i, 0)
    const = lambda i, f: (0, 0)
    return pl.pallas_call(
        _ffn_kernel,
        grid=(BS // tm, d_ff // tf),
        in_specs=[
            pl.BlockSpec((tm, D), row),
            pl.BlockSpec((tm, R_WIDTH), row),
            pl.BlockSpec((tm, N_WIDTH), row),
            pl.BlockSpec((R_WIDTH + N_WIDTH, D), const),
            pl.BlockSpec((1, D), const),
            pl.BlockSpec((D, tf), lambda i, f: (0, f)),
            pl.BlockSpec((D, tf), lambda i, f: (0, f)),
            pl.BlockSpec((tf, D), lambda i, f: (f, 0)),
            pl.BlockSpec((1, D), const),
        ],
        out_specs=pl.BlockSpec((tm, D), row),
        out_shape=jax.ShapeDtypeStruct((BS, D), F32),
        scratch_shapes=[pltpu.VMEM((tm, D), F32), pltpu.VMEM((tm, D), MXU_DTYPE), pltpu.VMEM((tm, D), F32)],
        compiler_params=pltpu.CompilerParams(
            dimension_semantics=("parallel", "arbitrary"), vmem_limit_bytes=VMEM_LIMIT),
        name="ffn",
    )(x2, y_r, y_n, w_out, nf, w_gate, w_up, w_down, nfin)


def _pad_to(a, n, axis):
    pad = [(0, 0)] * a.ndim
    pad[axis] = (0, n - a.shape[axis])
    return jnp.pad(a, pad)


def _rope_rows():
    lane = jnp.arange(LANES) % HEAD
    inv_freq = jnp.power(jnp.float32(ROPE_THETA), -jnp.arange(ROT_HALF, dtype=F32) * 2.0 / ROT_DIM)
    freq = jnp.where(lane < ROT_DIM, inv_freq[lane % ROT_HALF], 0.0).astype(F32)[None, :]
    sgn = jnp.where(lane < ROT_HALF, -1.0, jnp.where(lane < ROT_DIM, 1.0, 0.0)).astype(F32)[None, :]
    return freq, sgn


def _layer(h, positions, norm_mix, w_in, mu_shift, w0, w2, a0, a2, g2, k_k, k_a, r_k, lnx_w, lnx_b,
           pe_k, wk1, bk1, wk2, pe_v, wv1, bv1, wv2, w_out, norm_ffn, w_gate, w_up, w_down, norm_final):
    B, S, D = h.shape
    BS = B * S
    x2 = h.reshape(BS, D)
    freq, sgn = _rope_rows()
    row = lambda a: a.reshape(1, -1).astype(F32)

    w_r = _mx(_pad_to(w_in[:, :R_COLS], R_COLS_PAD, 1))
    w_n = _mx(_pad_to(w_in[:, R_COLS:], N_COLS_PAD, 1))
    tm = min(256, S)
    rw, q, kvc, ks, vst, kw, vwt, gt = _proj(
        x2, positions.reshape(BS, 1), row(norm_mix), freq, sgn, w_r, w_n, B, S, tm)

    T = min(256, S)
    head_of = jnp.arange(R_WIDTH) // HEAD
    gmat = (head_of[:, None] == head_of[None, :]).astype(MXU_DTYPE)
    tt = jnp.arange(T)
    ltri = ((tt[:, None] // CHUNK == tt[None, :] // CHUNK) & (tt[:, None] >= tt[None, :])).astype(MXU_DTYPE)
    w2p = _mx(_pad_to(w2, LANES, 0))
    a2p = _mx(jnp.concatenate([jnp.zeros((DECAY_LORA, R_WIDTH), F32), a2], axis=0))
    g2p = _mx(_pad_to(g2, GLAT_PAD, 0))
    y_r = _rwkv(rw, row(_pad_to(mu_shift, R_COLS_PAD, 0)), row(w0), w2p, row(a0), a2p, g2p,
                row(k_k), row(k_a), row(r_k), row(lnx_w), row(lnx_b), gmat, ltri, B, S, T)

    n_blk = S // CMP_STRIDE
    xblk = kvc.reshape(B, n_blk, CMP_STRIDE, 2 * N_KV_HEADS, HEAD).transpose(0, 3, 1, 2, 4)
    xblk = xblk.reshape(B, 2 * N_KV_HEADS, n_blk, CMP_STRIDE * HEAD)
    w1s = _mx(jnp.stack([wk1, wv1]))
    b1s = jnp.stack([bk1, bv1]).reshape(2, 1, CMP_HIDDEN).astype(F32)
    w2s = _mx(_pad_to(jnp.stack([wk2, wv2]), LANES, 2))
    pes = _pad_to(jnp.stack([pe_k, pe_v]).reshape(2, 1, CMP_BLOCK * HEAD), 8, 1).astype(F32)
    pos_c = _pad_to(positions[:, CMP_BLOCK - 1::CMP_STRIDE], n_blk, 1).reshape(B, n_blk, 1)
    cmp_rows, cmp_t = _compress(xblk, w1s, b1s, w2s, pes, pos_c, freq, sgn, B, n_blk)
    kc = cmp_rows[:, :N_KV_HEADS]
    vct = cmp_t[:, N_KV_HEADS:]

    y_n = _nsa(q, kc, vct, ks, vst, kw, vwt, gt, B, S)

    d_ff = w_gate.shape[1]
    tf = d_ff // 2 if (d_ff // 2) % LANES == 0 else d_ff
    out = _ffn(x2, y_r, y_n, _mx(w_out), row(norm_ffn), _mx(w_gate), _mx(w_up), _mx(w_down),
               row(norm_final), min(512, BS), tf)
    return out.reshape(B, S, D)


def kernel(x, positions, norm_mix, w_in, mu_shift, w0, w2, a0, a2, g2, k_k, k_a, r_k, lnx_w, lnx_b, pe_k, wk1, bk1, wk2, pe_v, wv1, bv1, wv2, w_out, norm_ffn, w_gate, w_up, w_down, norm_final):
    assert norm_mix.shape[0] == 1, "single-layer block"
    l = 0
    return _layer(x, positions, norm_mix[l], w_in[l], mu_shift[l], w0[l], w2[l], a0[l], a2[l], g2[l],
                  k_k[l], k_a[l], r_k[l].reshape(-1), lnx_w[l], lnx_b[l], pe_k[l], wk1[l], bk1[l], wk2[l],
                  pe_v[l], wv1[l], bv1[l], wv2[l], w_out[l], norm_ffn[l], w_gate[l], w_up[l], w_down[l],
                  norm_final)
```

```python
import functools

import jax
import jax.numpy as jnp
from jax import lax
from jax.experimental import pallas as pl
from jax.experimental.pallas import tpu as pltpu

F32 = jnp.float32
MXU_DTYPE = jnp.bfloat16

R_HEADS = 8
HEAD = 64
R_WIDTH = R_HEADS * HEAD
DECAY_LORA = 64
ICLR_LORA = 64
GATE_LORA = 160
LNX_EPS = 64e-5
R_COLS = 3 * R_WIDTH + DECAY_LORA + ICLR_LORA + GATE_LORA
N_Q_HEADS = 8
N_KV_HEADS = 2
GQA = N_Q_HEADS // N_KV_HEADS
N_WIDTH = N_Q_HEADS * HEAD
KV_WIDTH = N_KV_HEADS * HEAD
CMP_BLOCK = 32
CMP_STRIDE = 16
CMP_HIDDEN = 256
SEL_BLOCK = 64
SEL_TOPK = 16
WINDOW = 512
Q_BLOCK = 128
ROPE_THETA = 500000.0
ROT_DIM = HEAD // 4
ROT_HALF = ROT_DIM // 2
NORM_EPS = 1e-6
MASK = -1e30
FORCE = 1e6
REMOVED = -3e38

LANES = 128
R_COLS_PAD = 1920
GLAT_PAD = 256
N_COLS_PAD = N_WIDTH + 6 * KV_WIDTH + LANES
GATE_ROWS = 32
CHUNK = 64
SEL_GROUP = 8
GROUP = SEL_GROUP * SEL_BLOCK
KV_TILE = 256
ONES_ROWS = 16
LOG2E = 1.4426950408889634
VMEM_LIMIT = 56 * 1024 * 1024


def _dot(a, b):
    return jnp.dot(a, b, preferred_element_type=F32)


def _dot_nt(a, b):
    return lax.dot_general(a, b, (((1,), (1,)), ((), ())), preferred_element_type=F32)


def _mx(a):
    return a.astype(MXU_DTYPE)


def _rms(x, w):
    ms = jnp.mean(x * x, axis=-1, keepdims=True)
    return x * lax.rsqrt(ms + NORM_EPS) * w


def _rope_tables(pos_col, freq_row, sgn_row):
    ang = pos_col.astype(F32) * freq_row
    return jnp.cos(ang), jnp.sin(ang) * sgn_row


def _rope128(t, cos, sin, first):
    sh = jnp.where(first, pltpu.roll(t, LANES - ROT_HALF, 1), pltpu.roll(t, ROT_HALF, 1))
    return t * cos + sh * sin


def _proj_kernel(x_ref, pos_ref, nw_ref, freq_ref, sgn_ref, wr_ref, wn_ref,
                 rw_ref, q_ref, kvc_ref, ks_ref, vst_ref, kw_ref, vwt_ref, gt_ref):
    tm = x_ref.shape[0]
    xn = _mx(_rms(x_ref[...], nw_ref[...]))
    rw_ref[...] = _dot(xn, wr_ref[...])
    pn = _dot(xn, wn_ref[...])
    cos, sin = _rope_tables(pos_ref[...], freq_ref[...], sgn_ref[...])
    lane = lax.broadcasted_iota(jnp.int32, (1, LANES), 1)
    first = (lane % HEAD) < ROT_HALF
    rope = lambda t: _rope128(t, cos, sin, first)
    scale = HEAD ** -0.5 * LOG2E
    q = [rope(pn[:, i * LANES:(i + 1) * LANES]) * scale for i in range(N_WIDTH // LANES)]
    q_ref[...] = jnp.concatenate(q, axis=1).astype(q_ref.dtype)
    o = N_WIDTH
    kvc_ref[...] = pn[:, o:o + 2 * KV_WIDTH]
    o += 2 * KV_WIDTH

    row_g = pl.program_id(1) * tm + lax.broadcasted_iota(jnp.int32, (tm, 1), 0)
    onehot = jnp.where(lane == HEAD + (row_g // SEL_BLOCK) % SEL_GROUP, 1.0, 0.0)
    kk = rope(pn[:, o:o + KV_WIDTH])
    vt = rope(pn[:, o + KV_WIDTH:o + 2 * KV_WIDTH]).T
    o += 2 * KV_WIDTH
    ones_rows = jnp.where(lax.broadcasted_iota(jnp.int32, (ONES_ROWS, KV_TILE), 0) == 0, 1.0, 0.0)
    for h in range(N_KV_HEADS):
        k_h = kk if h == 0 else pltpu.roll(kk, HEAD, 1)
        ks_ref[h] = jnp.where(lane < HEAD, k_h, onehot).astype(ks_ref.dtype)
        for i in range(tm // KV_TILE):
            v_t = vt[h * HEAD:(h + 1) * HEAD, i * KV_TILE:(i + 1) * KV_TILE]
            vst_ref[h, i] = jnp.concatenate([v_t, ones_rows], axis=0).astype(vst_ref.dtype)

    kk = rope(pn[:, o:o + KV_WIDTH])
    vt = rope(pn[:, o + KV_WIDTH:o + 2 * KV_WIDTH]).T
    o += 2 * KV_WIDTH
    for h in range(N_KV_HEADS):
        kw_ref[h] = kk[:, h * HEAD:(h + 1) * HEAD].astype(kw_ref.dtype)
        for i in range(tm // LANES):
            vwt_ref[h, i] = vt[h * HEAD:(h + 1) * HEAD, i * LANES:(i + 1) * LANES].astype(vwt_ref.dtype)
    gt_ref[...] = jax.nn.sigmoid(pn[:, o:o + LANES]).T[:GATE_ROWS]


def _proj(x2, pos_col, nw, freq, sgn, w_r, w_n, B, S, tm):
    BS, D = x2.shape
    nt = S // tm
    nkt = S // LANES
    row = lambda b, i: (b * nt + i, 0)
    const = lambda b, i: (0, 0)
    head_rows = lambda width: pl.BlockSpec((None, N_KV_HEADS, tm, width), lambda b, i: (b, 0, i, 0))
    head_tiles = lambda rows, width: pl.BlockSpec(
        (None, N_KV_HEADS, tm // width, rows, width), lambda b, i: (b, 0, i, 0, 0))
    kv = lambda *shape: jax.ShapeDtypeStruct((B, N_KV_HEADS) + shape, MXU_DTYPE)
    return pl.pallas_call(
        _proj_kernel,
        grid=(B, nt),
        in_specs=[
            pl.BlockSpec((tm, D), row),
            pl.BlockSpec((tm, 1), row),
            pl.BlockSpec((1, D), const),
            pl.BlockSpec((1, LANES), const),
            pl.BlockSpec((1, LANES), const),
            pl.BlockSpec((D, R_COLS_PAD), const),
            pl.BlockSpec((D, N_COLS_PAD), const),
        ],
        out_specs=[
            pl.BlockSpec((tm, R_COLS_PAD), row),
            pl.BlockSpec((tm, N_WIDTH), row),
            pl.BlockSpec((tm, 2 * KV_WIDTH), row),
            head_rows(LANES), head_tiles(HEAD + ONES_ROWS, KV_TILE),
            head_rows(HEAD), head_tiles(HEAD, LANES),
            pl.BlockSpec((None, GATE_ROWS, tm), lambda b, i: (b, 0, i)),
        ],
        out_shape=[
            jax.ShapeDtypeStruct((BS, R_COLS_PAD), F32),
            jax.ShapeDtypeStruct((BS, N_WIDTH), MXU_DTYPE),
            jax.ShapeDtypeStruct((BS, 2 * KV_WIDTH), F32),
            kv(S, LANES), kv(S // KV_TILE, HEAD + ONES_ROWS, KV_TILE),
            kv(S, HEAD), kv(nkt, HEAD, LANES),
            jax.ShapeDtypeStruct((B, GATE_ROWS, S), F32),
        ],
        compiler_params=pltpu.CompilerParams(
            dimension_semantics=("parallel", "parallel"), vmem_limit_bytes=VMEM_LIMIT),
        name="proj",
    )(x2, pos_col, nw, freq, sgn, w_r, w_n)


def _rwkv_kernel(rw_ref, mu_ref, w0_ref, w2_ref, a0_ref, a2_ref, g2_ref, kk_ref, ka_ref, rk_ref,
                 lw_ref, lb_ref, gmat_ref, ltri_ref, y_ref,
                 state_s, prev_s, at_s, rt_s, bt_s, kt_s, v_s, wc_s, y_s, bonus_s, g_s):
    T = rw_ref.shape[0]

    @pl.when(pl.program_id(1) == 0)
    def _():
        state_s[...] = jnp.zeros_like(state_s)
        prev_s[...] = jnp.zeros_like(prev_s)

    x = rw_ref[...]
    row = lax.broadcasted_iota(jnp.int32, (T, 1), 0)
    xs = jnp.where(row == 0, prev_s[0:1, :], pltpu.roll(x, 1, 0))
    prev_s[0:1, :] = x[T - 1:T, :]
    xm = x + (xs - x) * mu_ref[...]
    r = xm[:, 0:R_WIDTH]
    k = xm[:, R_WIDTH:2 * R_WIDTH]
    v = xm[:, 2 * R_WIDTH:3 * R_WIDTH]
    lat = xm[:, 3 * R_WIDTH:3 * R_WIDTH + LANES]
    glat = xm[:, 3 * R_WIDTH + LANES:3 * R_WIDTH + LANES + GLAT_PAD]

    gmat = gmat_ref[...]

    def gsum(t):
        hi = _mx(t)
        lo = _mx(t - hi.astype(F32))
        return _dot(hi, gmat) + _dot(lo, gmat)

    wv = w0_ref[...] + _dot(_mx(jnp.tanh(lat)), w2_ref[...])
    w = -(jnp.maximum(-wv, 0.0) + jnp.log(1.0 + jnp.exp(-jnp.abs(wv)))) - 0.5
    ld = -jnp.exp(w)
    a = jax.nn.sigmoid(a0_ref[...] + _dot(_mx(lat), a2_ref[...]))
    g_s[...] = _dot(_mx(jax.nn.sigmoid(glat)), g2_ref[...])
    kkv = k * kk_ref[...]
    kkn = kkv / jnp.maximum(jnp.sqrt(gsum(kkv * kkv)), 1e-12)
    k_mod = k * (1.0 + (a - 1.0) * ka_ref[...])
    bonus_s[...] = gsum(r * k_mod * rk_ref[...]) * v

    ltri = ltri_ref[...]
    h1 = _mx(ld)
    r1 = ld - h1.astype(F32)
    h2 = _mx(r1)
    h3 = _mx(r1 - h2.astype(F32))
    cum = _dot(ltri, h1) + _dot(ltri, h2) + _dot(ltri, h3)
    w_inc = jnp.exp(cum)
    w_inv = jnp.exp(-cum)
    w_exc = jnp.exp(cum - ld)
    at_s[...] = (-kkn * w_exc).astype(at_s.dtype)
    rt_s[...] = (r * w_inc).astype(rt_s.dtype)
    bt_s[...] = (kkn * a * w_inv).astype(bt_s.dtype)
    kt_s[...] = (k_mod * w_inv).astype(kt_s.dtype)
    v_s[...] = v
    wc_s[...] = w_inc

    ii = lax.broadcasted_iota(jnp.int32, (CHUNK, CHUNK), 0)
    jj = lax.broadcasted_iota(jnp.int32, (CHUNK, CHUNK), 1)
    strict = ii > jj
    incl = ii >= jj
    eye = (ii == jj).astype(F32)

    def chunk_body(c, carry):
        r0 = pl.multiple_of(c * CHUNK, CHUNK)
        rows = pl.ds(r0, CHUNK)
        hs = range(R_HEADS)
        sl = [slice(h * HEAD, (h + 1) * HEAD) for h in hs]
        a_ = [at_s[rows, sl[h]] for h in hs]
        r_ = [rt_s[rows, sl[h]] for h in hs]
        b_ = [bt_s[rows, sl[h]] for h in hs]
        k_ = [kt_s[rows, sl[h]] for h in hs]
        vf = [v_s[rows, sl[h]] for h in hs]
        v_ = [_mx(t) for t in vf]
        ar = [jnp.concatenate([a_[h], r_[h]], axis=0) for h in hs]
        bk = [jnp.concatenate([b_[h], k_[h]], axis=0) for h in hs]
        mb = [_dot_nt(ar[h], b_[h]) for h in hs]
        mk = [_dot_nt(ar[h], k_[h]) for h in hs]
        s0 = [state_s[h] for h in hs]
        asrs = [_dot_nt(ar[h], _mx(s0[h])) for h in hs]
        a_ab = [jnp.where(strict, mb[h][:CHUNK], 0.0) for h in hs]
        a_rb = [_mx(jnp.where(incl, mb[h][CHUNK:], 0.0)) for h in hs]
        a_ak = [_mx(jnp.where(strict, mk[h][:CHUNK], 0.0)) for h in hs]
        a_rk = [_mx(jnp.where(incl, mk[h][CHUNK:], 0.0)) for h in hs]
        ab = [_mx(t) for t in a_ab]
        xp = [_dot(ab[h], ab[h]) for h in hs]
        qs = [eye + a_ab[h] for h in hs]
        akv = [_dot(a_ak[h], v_[h]) for h in hs]
        yk = [_dot(a_rk[h], v_[h]) for h in hs]
        n = 2
        while n < CHUNK:
            xb = [_mx(t) for t in xp]
            if 2 * n < CHUNK:
                res = [_dot(_mx(jnp.concatenate([xp[h], qs[h]], axis=0)), xb[h]) for h in hs]
                xp = [res[h][:CHUNK] for h in hs]
                qs = [qs[h] + res[h][CHUNK:] for h in hs]
            else:
                upd = [_dot(_mx(qs[h]), xb[h]) for h in hs]
                qs = [qs[h] + upd[h] for h in hs]
            n *= 2
        rhs = [_mx(asrs[h][:CHUNK] + akv[h]) for h in hs]
        u = [_dot(_mx(qs[h]), rhs[h]) for h in hs]
        yb = [_dot(a_rb[h], _mx(u[h])) for h in hs]
        uv_t = [_mx(jnp.concatenate([u[h], vf[h]], axis=0).T) for h in hs]
        ds = [_dot(uv_t[h], bk[h]) for h in hs]
        for h in hs:
            wc = wc_s[pl.ds(r0 + CHUNK - 8, 8), sl[h]][7:8]
            state_s[h] = (s0[h] + ds[h]) * wc
            y_s[rows, sl[h]] = asrs[h][CHUNK:] + yb[h] + yk[h]
        return carry

    lax.fori_loop(0, T // CHUNK, chunk_body, 0)

    y = y_s[...]
    mean = gsum(y) * (1.0 / HEAD)
    d = y - mean
    var = gsum(d * d) * (1.0 / HEAD)
    yn = d * lax.rsqrt(var + LNX_EPS) * lw_ref[...] + lb_ref[...]
    y_ref[...] = ((yn + bonus_s[...]) * g_s[...]).astype(y_ref.dtype)


def _rwkv(rw, mu, w0, w2p, a0, a2p, g2p, k_k, k_a, r_k, lnx_w, lnx_b, gmat, ltri, B, S, T):
    BS = rw.shape[0]
    nt = S // T
    const = lambda b, i: (0, 0)
    vec = pl.BlockSpec((1, R_WIDTH), const)
    big = lambda dt: pltpu.VMEM((T, R_WIDTH), dt)
    return pl.pallas_call(
        _rwkv_kernel,
        grid=(B, nt),
        in_specs=[
            pl.BlockSpec((T, R_COLS_PAD), lambda b, i: (b * nt + i, 0)),
            pl.BlockSpec((1, R_COLS_PAD), const),
            vec,
            pl.BlockSpec((LANES, R_WIDTH), const),
            vec,
            pl.BlockSpec((LANES, R_WIDTH), const),
            pl.BlockSpec((GLAT_PAD, R_WIDTH), const),
            vec, vec, vec, vec, vec,
            pl.BlockSpec((R_WIDTH, R_WIDTH), const),
            pl.BlockSpec((T, T), const),
        ],
        out_specs=pl.BlockSpec((T, R_WIDTH), lambda b, i: (b * nt + i, 0)),
        out_shape=jax.ShapeDtypeStruct((BS, R_WIDTH), MXU_DTYPE),
        scratch_shapes=[
            pltpu.VMEM((R_HEADS, HEAD, HEAD), F32),
            pltpu.VMEM((8, R_COLS_PAD), F32),
            big(MXU_DTYPE), big(MXU_DTYPE), big(MXU_DTYPE), big(MXU_DTYPE),
            big(F32), big(F32), big(F32), big(F32), big(F32),
        ],
        compiler_params=pltpu.CompilerParams(
            dimension_semantics=("parallel", "arbitrary"), vmem_limit_bytes=VMEM_LIMIT),
        name="rwkv",
    )(rw, mu, w0, w2p, a0, a2p, g2p, k_k, k_a, r_k, lnx_w, lnx_b, gmat, ltri)


def _cmp_kernel(x_ref, w1_ref, b1_ref, w2_ref, pe_ref, pos_ref, freq_ref, sgn_ref, o_ref, ot_ref):
    n_blk = x_ref.shape[0]
    half = x_ref.shape[1]
    j = pl.program_id(1)
    xb = _mx(x_ref[...])
    w1 = w1_ref[...]
    first_half = _dot(xb, w1[:half])
    second_half = pltpu.roll(_dot(xb, w1[half:]), n_blk - 1, 0)
    bias = _dot(_mx(pe_ref[...]), w1)[0:1] + b1_ref[...]
    hdn = first_half + second_half + bias
    hdn = hdn * jax.nn.sigmoid(hdn)
    out = _dot(_mx(hdn), w2_ref[...])
    cos, sin = _rope_tables(pos_ref[...], freq_ref[...], sgn_ref[...])
    lane = lax.broadcasted_iota(jnp.int32, (1, LANES), 1)
    roped = _rope128(out, cos, sin, (lane % HEAD) < ROT_HALF)
    res = jnp.where(j < N_KV_HEADS, roped, out)
    row = lax.broadcasted_iota(jnp.int32, (n_blk, 1), 0)
    res = jnp.where(row == n_blk - 1, 0.0, res)
    o_ref[...] = res[:, :HEAD].astype(o_ref.dtype)
    ot_ref[...] = res.T[:HEAD].astype(ot_ref.dtype)


def _compress(xblk, w1s, b1s, w2s, pes, pos_c, freq, sgn, B, n_blk):
    half = xblk.shape[-1]
    return pl.pallas_call(
        _cmp_kernel,
        grid=(B, 2 * N_KV_HEADS),
        in_specs=[
            pl.BlockSpec((None, None, n_blk, half), lambda b, j: (b, j, 0, 0)),
            pl.BlockSpec((None, 2 * half, CMP_HIDDEN), lambda b, j: (j // N_KV_HEADS, 0, 0)),
            pl.BlockSpec((None, 1, CMP_HIDDEN), lambda b, j: (j // N_KV_HEADS, 0, 0)),
            pl.BlockSpec((None, CMP_HIDDEN, LANES), lambda b, j: (j // N_KV_HEADS, 0, 0)),
            pl.BlockSpec((None, 8, 2 * half), lambda b, j: (j // N_KV_HEADS, 0, 0)),
            pl.BlockSpec((None, n_blk, 1), lambda b, j: (b, 0, 0)),
            pl.BlockSpec((1, LANES), lambda b, j: (0, 0)),
            pl.BlockSpec((1, LANES), lambda b, j: (0, 0)),
        ],
        out_specs=[
            pl.BlockSpec((None, None, n_blk, HEAD), lambda b, j: (b, j, 0, 0)),
            pl.BlockSpec((None, None, HEAD, n_blk), lambda b, j: (b, j, 0, 0)),
        ],
        out_shape=[
            jax.ShapeDtypeStruct((B, 2 * N_KV_HEADS, n_blk, HEAD), MXU_DTYPE),
            jax.ShapeDtypeStruct((B, 2 * N_KV_HEADS, HEAD, n_blk), MXU_DTYPE),
        ],
        compiler_params=pltpu.CompilerParams(
            dimension_semantics=("parallel", "parallel"), vmem_limit_bytes=VMEM_LIMIT),
        name="compress",
    )(xblk, w1s, b1s, w2s, pes, pos_c, freq, sgn)


def _nsa_kernel(q_ref, kc_ref, vct_ref, ks_ref, vst_ref, kw_ref, vwt_ref, g_ref, o_ref,
                sel_s, ocmp_s, sbuf_s, *, n_top, levels):
    n_blk = kc_ref.shape[0]
    n_sel = sel_s.shape[0]
    hk = pl.program_id(1)
    qb = pl.program_id(2)
    start = qb * Q_BLOCK
    W = GQA * Q_BLOCK

    q_t = q_ref[...].astype(F32).T
    qt = _mx(jnp.concatenate([q_t[g * HEAD:(g + 1) * HEAD] for g in range(GQA)], axis=1))
    tile4 = lambda t: jnp.concatenate([t] * GQA, axis=1)
    qi = lax.broadcasted_iota(jnp.int32, (1, Q_BLOCK), 1)
    t_q = start + qi
    t_w = tile4(t_q)

    def compressed_and_select(n_c, n_s):
        c_end = lax.broadcasted_iota(jnp.int32, (n_c, 1), 0) * CMP_STRIDE + (CMP_BLOCK - 1)
        sc = _dot(kc_ref[0:n_c, :], qt) + tile4(jnp.where(c_end <= t_q, 0.0, MASK))
        m = jnp.max(sc, axis=0, keepdims=True)
        p = jnp.exp2(sc - m)
        l = jnp.sum(p, axis=0, keepdims=True)
        pn = p * jnp.where(t_w >= CMP_BLOCK - 1, 1.0 / l, 0.0)
        ocmp_s[...] = _dot(vct_ref[:, 0:n_c], _mx(pn))

        ps = pn[:, 0:Q_BLOCK]
        for g in range(1, GQA):
            ps = ps + pn[:, g * Q_BLOCK:(g + 1) * Q_BLOCK]
        hi = _mx(ps)
        lo = _mx(ps - hi.astype(F32))
        j_col = lax.broadcasted_iota(jnp.int32, (n_s, 1), 0)
        c_row = lax.broadcasted_iota(jnp.int32, (1, n_c), 1)
        cover = ((c_row * CMP_STRIDE < j_col * SEL_BLOCK + SEL_BLOCK)
                 & (c_row * CMP_STRIDE + CMP_BLOCK > j_col * SEL_BLOCK))
        cover = jnp.where(cover, 1.0, 0.0).astype(MXU_DTYPE)
        imp = _dot(cover, hi) + _dot(cover, lo)

        cur = t_q // SEL_BLOCK
        forced = (j_col == 0) | (j_col == cur) | (j_col == cur - 1)
        score = jnp.where(forced, REMOVED, jnp.where(j_col <= cur, imp, MASK))
        sel = jnp.where(forced, 1.0, 0.0)
        for _ in range(max(n_top - 3, 0)):
            mx = jnp.max(score, axis=0, keepdims=True)
            idx = jnp.min(jnp.where(score == mx, j_col, n_s), axis=0, keepdims=True)
            pick = j_col == idx
            sel = jnp.where(pick, 1.0, sel)
            score = jnp.where(pick, REMOVED, score)
        sel_s[0:n_s, :] = sel

    lo_qb = 0
    for hi_qb in levels:
        n_c = min(hi_qb * (Q_BLOCK // CMP_STRIDE), n_blk)
        n_s = min(hi_qb * (Q_BLOCK // SEL_BLOCK), n_sel)
        pl.when((qb >= lo_qb) & (qb < hi_qb))(functools.partial(compressed_and_select, n_c, n_s))
        lo_qb = hi_qb
    o_cmp = ocmp_s[...]

    tiles = GROUP // KV_TILE
    zero_rows = jnp.zeros((LANES - HEAD - 2 * SEL_GROUP, W), MXU_DTYPE)

    def group_scores(j):
        grp = sel_s[pl.ds(pl.multiple_of(j * SEL_GROUP, SEL_GROUP), SEL_GROUP), :]
        neg = jnp.where(grp > 0.5, 0.0, MASK)
        neg = tile4(jnp.concatenate([neg, jnp.zeros_like(neg)], axis=0))
        q_aug = jnp.concatenate([qt, _mx(neg), zero_rows], axis=0)
        r0 = pl.multiple_of(j * GROUP, GROUP)
        return [_dot(ks_ref[pl.ds(r0 + i * KV_TILE, KV_TILE), :], q_aug) for i in range(tiles)]

    def put_scores(slot, j):
        for i, s in enumerate(group_scores(j)):
            sbuf_s[slot, i * KV_TILE:(i + 1) * KV_TILE, :] = s

    def online(s, vt_tile, carry):
        m_, acc = carry
        m_new = jnp.maximum(m_, jnp.max(s, axis=0, keepdims=True))
        alpha = jnp.exp2(m_ - m_new)
        return m_new, alpha * acc + _dot(vt_tile, _mx(jnp.exp2(s - m_new)))

    def consume(slot, j, carry):
        for i in range(tiles):
            s = sbuf_s[slot, i * KV_TILE:(i + 1) * KV_TILE, :]
            carry = online(s, vst_ref[j * tiles + i], carry)
        return carry

    def pair_body(jp, carry):
        j = 2 * jp
        put_scores(1, j + 1)
        carry = consume(0, j, carry)
        put_scores(0, j + 2)
        return consume(1, j + 1, carry)

    def single_body(j, carry):
        carry = consume(0, j, carry)
        put_scores(0, j + 1)
        return carry

    init = (jnp.full((1, W), MASK, F32), jnp.zeros((HEAD + ONES_ROWS, W), F32))
    last = qb // (GROUP // Q_BLOCK)
    put_scores(0, 0)
    quad_body = lambda jq, c: pair_body(2 * jq + 1, pair_body(2 * jq, c))
    carry = lax.fori_loop(0, last // 4, quad_body, init)
    carry = lax.fori_loop(2 * (last // 4), last // 2, pair_body, carry)
    carry = lax.fori_loop(2 * (last // 2), last, single_body, carry)

    w0 = pl.multiple_of(jnp.maximum(start - WINDOW, 0), Q_BLOCK)
    n_win = WINDOW + Q_BLOCK
    s_win = _dot(kw_ref[pl.ds(w0, n_win), :], qt)

    k_row = lax.broadcasted_iota(jnp.int32, (KV_TILE, 1), 0)
    for i in range(tiles):
        tok = last * GROUP + i * KV_TILE + k_row
        causal = tile4(jnp.where(tok <= t_q, 0.0, MASK))
        s = sbuf_s[0, i * KV_TILE:(i + 1) * KV_TILE, :] + causal
        carry = online(s, vst_ref[last * tiles + i], carry)
    _, acc_sel = carry
    o_sel = acc_sel[:HEAD] * (1.0 / acc_sel[HEAD:HEAD + 1])

    tok = w0 + lax.broadcasted_iota(jnp.int32, (n_win, 1), 0)
    in_win = (tok <= t_q) & (tok > t_q - WINDOW)
    s_win = s_win + tile4(jnp.where(in_win, 0.0, MASK))
    p_win = jnp.exp2(s_win - jnp.max(s_win, axis=0, keepdims=True))
    l_win = jnp.sum(p_win, axis=0, keepdims=True)
    p_win = _mx(p_win)
    acc_win = jnp.zeros((HEAD, W), F32)
    for i in range(n_win // Q_BLOCK):
        acc_win = acc_win + _dot(vwt_ref[w0 // Q_BLOCK + i], p_win[i * Q_BLOCK:(i + 1) * Q_BLOCK])
    o_win = acc_win * (1.0 / l_win)

    outs = []
    gates = g_ref[...]
    per_kv = GQA * 3
    gate = lambda r: jnp.where(hk == 0, gates[r:r + 1], gates[per_kv + r:per_kv + r + 1])
    for g in range(GQA):
        cols = slice(g * Q_BLOCK, (g + 1) * Q_BLOCK)
        o_g = (gate(g * 3) * o_cmp[:, cols] + gate(g * 3 + 1) * o_sel[:, cols]
               + gate(g * 3 + 2) * o_win[:, cols])
        outs.append(o_g)
    o_ref[...] = jnp.concatenate(outs, axis=0).T.astype(o_ref.dtype)


def _nsa(q, kc, vct, ks, vst, kw, vwt, gt, B, S):
    BS = q.shape[0]
    nqb = S // Q_BLOCK
    n_blk = S // CMP_STRIDE
    n_sel = S // SEL_BLOCK
    n_top = min(SEL_TOPK, n_sel)
    levels = tuple(b for b in (nqb // 4, nqb // 2) if b > 0) + (nqb,)
    per_head = lambda *blk: pl.BlockSpec((None, None) + blk, lambda b, h, i: (b, h) + (0,) * len(blk))
    qspec = pl.BlockSpec((Q_BLOCK, GQA * HEAD), lambda b, h, i: (b * nqb + i, h))
    return pl.pallas_call(
        functools.partial(_nsa_kernel, n_top=n_top, levels=levels),
        grid=(B, N_KV_HEADS, nqb),
        in_specs=[
            qspec,
            per_head(n_blk, HEAD),
            per_head(HEAD, n_blk),
            per_head(S, LANES),
            per_head(S // KV_TILE, HEAD + ONES_ROWS, KV_TILE),
            per_head(S, HEAD),
            per_head(nqb, HEAD, LANES),
            pl.BlockSpec((None, GATE_ROWS, Q_BLOCK), lambda b, h, i: (b, 0, i)),
        ],
        out_specs=qspec,
        out_shape=jax.ShapeDtypeStruct((BS, N_WIDTH), MXU_DTYPE),
        scratch_shapes=[
            pltpu.VMEM((n_sel, Q_BLOCK), F32),
            pltpu.VMEM((HEAD, GQA * Q_BLOCK), F32),
            pltpu.VMEM((2, GROUP, GQA * Q_BLOCK), F32),
        ],
        compiler_params=pltpu.CompilerParams(
            dimension_semantics=("parallel", "parallel", "arbitrary"), vmem_limit_bytes=VMEM_LIMIT),
        name="nsa",
    )(q, kc, vct, ks, vst, kw, vwt, gt)


def _ffn_kernel(x_ref, yr_ref, yn_ref, wo_ref, nf_ref, wg_ref, wu_ref, wd_ref, nfin_ref, o_ref,
                h1_s, xn_s, acc_s):
    f = pl.program_id(1)

    @pl.when(f == 0)
    def _():
        mix = _dot(yr_ref[...], wo_ref[:R_WIDTH]) + _dot(yn_ref[...], wo_ref[R_WIDTH:])
        h1 = x_ref[...] + mix
        h1_s[...] = h1
        xn_s[...] = _rms(h1, nf_ref[...]).astype(xn_s.dtype)
        acc_s[...] = jnp.zeros_like(acc_s)

    xn = xn_s[...]
    gate = _dot(xn, wg_ref[...])
    up = _dot(xn, wu_ref[...])
    acc_s[...] += _dot(_mx(gate * jax.nn.sigmoid(gate) * up), wd_ref[...])

    @pl.when(f == pl.num_programs(1) - 1)
    def _():
        o_ref[...] = _rms(h1_s[...] + acc_s[...], nfin_ref[...])


def _ffn(x2, y_r, y_n, w_out, nf, w_gate, w_up, w_down, nfin, tm, tf):
    BS, D = x2.shape
    d_ff = w_gate.shape[1]
    row = lambda i, f: (i, 0)
    const = lambda i, f: (0, 0)
    return pl.pallas_call(
        _ffn_kernel,
        grid=(BS // tm, d_ff // tf),
        in_specs=[
            pl.BlockSpec((tm, D), row),
            pl.BlockSpec((tm, R_WIDTH), row),
            pl.BlockSpec((tm, N_WIDTH), row),
            pl.BlockSpec((R_WIDTH + N_WIDTH, D), const),
            pl.BlockSpec((1, D), const),
            pl.BlockSpec((D, tf), lambda i, f: (0, f)),
            pl.BlockSpec((D, tf), lambda i, f: (0, f)),
            pl.BlockSpec((tf, D), lambda i, f: (f, 0)),
            pl.BlockSpec((1, D), const),
        ],
        out_specs=pl.BlockSpec((tm, D), row),
        out_shape=jax.ShapeDtypeStruct((BS, D), F32),
        scratch_shapes=[pltpu.VMEM((tm, D), F32), pltpu.VMEM((tm, D), MXU_DTYPE), pltpu.VMEM((tm, D), F32)],
        compiler_params=pltpu.CompilerParams(
            dimension_semantics=("parallel", "arbitrary"), vmem_limit_bytes=VMEM_LIMIT),
        name="ffn",
    )(x2, y_r, y_n, w_out, nf, w_gate, w_up, w_down, nfin)


def _pad_to(a, n, axis):
    pad = [(0, 0)] * a.ndim
    pad[axis] = (0, n - a.shape[axis])
    return jnp.pad(a, pad)


def _rope_rows():
    lane = jnp.arange(LANES) % HEAD
    inv_freq = jnp.power(jnp.float32(ROPE_THETA), -jnp.arange(ROT_HALF, dtype=F32) * 2.0 / ROT_DIM)
    freq = jnp.where(lane < ROT_DIM, inv_freq[lane % ROT_HALF], 0.0).astype(F32)[None, :]
    sgn = jnp.where(lane < ROT_HALF, -1.0, jnp.where(lane < ROT_DIM, 1.0, 0.0)).astype(F32)[None, :]
    return freq, sgn


def _layer(h, positions, norm_mix, w_in, mu_shift, w0, w2, a0, a2, g2, k_k, k_a, r_k, lnx_w, lnx_b,
           pe_k, wk1, bk1, wk2, pe_v, wv1, bv1, wv2, w_out, norm_ffn, w_gate, w_up, w_down, norm_final):
    B, S, D = h.shape
    BS = B * S
    x2 = h.reshape(BS, D)
    freq, sgn = _rope_rows()
    row = lambda a: a.reshape(1, -1).astype(F32)

    w_r = _mx(_pad_to(w_in[:, :R_COLS], R_COLS_PAD, 1))
    w_n = _mx(_pad_to(w_in[:, R_COLS:], N_COLS_PAD, 1))
    tm = min(256, S)
    rw, q, kvc, ks, vst, kw, vwt, gt = _proj(
        x2, positions.reshape(BS, 1), row(norm_mix), freq, sgn, w_r, w_n, B, S, tm)

    T = min(256, S)
    head_of = jnp.arange(R_WIDTH) // HEAD
    gmat = (head_of[:, None] == head_of[None, :]).astype(MXU_DTYPE)
    tt = jnp.arange(T)
    ltri = ((tt[:, None] // CHUNK == tt[None, :] // CHUNK) & (tt[:, None] >= tt[None, :])).astype(MXU_DTYPE)
    w2p = _mx(_pad_to(w2, LANES, 0))
    a2p = _mx(jnp.concatenate([jnp.zeros((DECAY_LORA, R_WIDTH), F32), a2], axis=0))
    g2p = _mx(_pad_to(g2, GLAT_PAD, 0))
    y_r = _rwkv(rw, row(_pad_to(mu_shift, R_COLS_PAD, 0)), row(w0), w2p, row(a0), a2p, g2p,
                row(k_k), row(k_a), row(r_k), row(lnx_w), row(lnx_b), gmat, ltri, B, S, T)

    n_blk = S // CMP_STRIDE
    xblk = kvc.reshape(B, n_blk, CMP_STRIDE, 2 * N_KV_HEADS, HEAD).transpose(0, 3, 1, 2, 4)
    xblk = xblk.reshape(B, 2 * N_KV_HEADS, n_blk, CMP_STRIDE * HEAD)
    w1s = _mx(jnp.stack([wk1, wv1]))
    b1s = jnp.stack([bk1, bv1]).reshape(2, 1, CMP_HIDDEN).astype(F32)
    w2s = _mx(_pad_to(jnp.stack([wk2, wv2]), LANES, 2))
    pes = _pad_to(jnp.stack([pe_k, pe_v]).reshape(2, 1, CMP_BLOCK * HEAD), 8, 1).astype(F32)
    pos_c = _pad_to(positions[:, CMP_BLOCK - 1::CMP_STRIDE], n_blk, 1).reshape(B, n_blk, 1)
    cmp_rows, cmp_t = _compress(xblk, w1s, b1s, w2s, pes, pos_c, freq, sgn, B, n_blk)
    kc = cmp_rows[:, :N_KV_HEADS]
    vct = cmp_t[:, N_KV_HEADS:]

    y_n = _nsa(q, kc, vct, ks, vst, kw, vwt, gt, B, S)

    d_ff = w_gate.shape[1]
    tf = d_ff // 2 if (d_ff // 2) % LANES == 0 else d_ff
    out = _ffn(x2, y_r, y_n, _mx(w_out), row(norm_ffn), _mx(w_gate), _mx(w_up), _mx(w_down),
               row(norm_final), min(512, BS), tf)
    return out.reshape(B, S, D)


def kernel(x, positions, norm_mix, w_in, mu_shift, w0, w2, a0, a2, g2, k_k, k_a, r_k, lnx_w, lnx_b, pe_k, wk1, bk1, wk2, pe_v, wv1, bv1, wv2, w_out, norm_ffn, w_gate, w_up, w_down, norm_final):
    assert norm_mix.shape[0] == 1, "single-layer block"
    l = 0
    return _layer(x, positions, norm_mix[l], w_in[l], mu_shift[l], w0[l], w2[l], a0[l], a2[l], g2[l],
                  k_k[l], k_a[l], r_k[l].reshape(-1), lnx_w[l], lnx_b[l], pe_k[l], wk1[l], bk1[l], wk2[l],
                  pe_v[l], wv1[l], bv1[l], wv2[l], w_out[l], norm_ffn[l], w_gate[l], w_up[l], w_down[l],
                  norm_final)
```

```python
import functools

import jax
import jax.numpy as jnp
from jax import lax
from jax.experimental import pallas as pl
from jax.experimental.pallas import tpu as pltpu

F32 = jnp.float32
MXU_DTYPE = jnp.bfloat16

R_HEADS = 8
HEAD = 64
R_WIDTH = R_HEADS * HEAD
DECAY_LORA = 64
ICLR_LORA = 64
GATE_LORA = 160
LNX_EPS = 64e-5
R_COLS = 3 * R_WIDTH + DECAY_LORA + ICLR_LORA + GATE_LORA
N_Q_HEADS = 8
N_KV_HEADS = 2
GQA = N_Q_HEADS // N_KV_HEADS
N_WIDTH = N_Q_HEADS * HEAD
KV_WIDTH = N_KV_HEADS * HEAD
CMP_BLOCK = 32
CMP_STRIDE = 16
CMP_HIDDEN = 256
SEL_BLOCK = 64
SEL_TOPK = 16
WINDOW = 512
Q_BLOCK = 128
ROPE_THETA = 500000.0
ROT_DIM = HEAD // 4
ROT_HALF = ROT_DIM // 2
NORM_EPS = 1e-6
MASK = -1e30
FORCE = 1e6
REMOVED = -3e38

LANES = 128
R_COLS_PAD = 1920
GLAT_PAD = 256
N_COLS_PAD = N_WIDTH + 6 * KV_WIDTH + LANES
GATE_ROWS = 32
CHUNK = 64
SLAB = 4 * HEAD
SEL_GROUP = 8
GROUP = SEL_GROUP * SEL_BLOCK
KV_TILE = 256
ONES_ROWS = 16
LOG2E = 1.4426950408889634
VMEM_LIMIT = 56 * 1024 * 1024


def _dot(a, b):
    return jnp.dot(a, b, preferred_element_type=F32)


def _dot_nt(a, b):
    return lax.dot_general(a, b, (((1,), (1,)), ((), ())), preferred_element_type=F32)


def _mx(a):
    return a.astype(MXU_DTYPE)


def _rms(x, w):
    ms = jnp.mean(x * x, axis=-1, keepdims=True)
    return x * lax.rsqrt(ms + NORM_EPS) * w


def _rope_tables(pos_col, freq_row, sgn_row):
    ang = pos_col.astype(F32) * freq_row
    return jnp.cos(ang), jnp.sin(ang) * sgn_row


def _rope128(t, cos, sin, first):
    sh = jnp.where(first, pltpu.roll(t, LANES - ROT_HALF, 1), pltpu.roll(t, ROT_HALF, 1))
    return t * cos + sh * sin


def _proj_kernel(x_ref, pos_ref, nw_ref, freq_ref, sgn_ref, wr_ref, wn_ref,
                 rw_ref, q_ref, kvc_ref, ks_ref, vst_ref, kw_ref, vwt_ref, gt_ref):
    tm = x_ref.shape[0]
    xn = _mx(_rms(x_ref[...], nw_ref[...]))
    rw_ref[...] = _dot(xn, wr_ref[...])
    pn = _dot(xn, wn_ref[...])
    cos, sin = _rope_tables(pos_ref[...], freq_ref[...], sgn_ref[...])
    lane = lax.broadcasted_iota(jnp.int32, (1, LANES), 1)
    first = (lane % HEAD) < ROT_HALF
    rope = lambda t: _rope128(t, cos, sin, first)
    scale = HEAD ** -0.5 * LOG2E
    q = [rope(pn[:, i * LANES:(i + 1) * LANES]) * scale for i in range(N_WIDTH // LANES)]
    q_ref[...] = jnp.concatenate(q, axis=1).astype(q_ref.dtype)
    o = N_WIDTH
    kvc_ref[...] = pn[:, o:o + 2 * KV_WIDTH]
    o += 2 * KV_WIDTH

    row_g = pl.program_id(1) * tm + lax.broadcasted_iota(jnp.int32, (tm, 1), 0)
    onehot = jnp.where(lane == HEAD + (row_g // SEL_BLOCK) % SEL_GROUP, 1.0, 0.0)
    kk = rope(pn[:, o:o + KV_WIDTH])
    vt = rope(pn[:, o + KV_WIDTH:o + 2 * KV_WIDTH]).T
    o += 2 * KV_WIDTH
    ones_rows = jnp.where(lax.broadcasted_iota(jnp.int32, (ONES_ROWS, KV_TILE), 0) == 0, 1.0, 0.0)
    for h in range(N_KV_HEADS):
        k_h = kk if h == 0 else pltpu.roll(kk, HEAD, 1)
        ks_ref[h] = jnp.where(lane < HEAD, k_h, onehot).astype(ks_ref.dtype)
        for i in range(tm // KV_TILE):
            v_t = vt[h * HEAD:(h + 1) * HEAD, i * KV_TILE:(i + 1) * KV_TILE]
            vst_ref[h, i] = jnp.concatenate([v_t, ones_rows], axis=0).astype(vst_ref.dtype)

    kk = rope(pn[:, o:o + KV_WIDTH])
    vt = rope(pn[:, o + KV_WIDTH:o + 2 * KV_WIDTH]).T
    o += 2 * KV_WIDTH
    for h in range(N_KV_HEADS):
        kw_ref[h] = kk[:, h * HEAD:(h + 1) * HEAD].astype(kw_ref.dtype)
        for i in range(tm // LANES):
            vwt_ref[h, i] = vt[h * HEAD:(h + 1) * HEAD, i * LANES:(i + 1) * LANES].astype(vwt_ref.dtype)
    gt_ref[...] = jax.nn.sigmoid(pn[:, o:o + LANES]).T[:GATE_ROWS]


def _proj(x2, pos_col, nw, freq, sgn, w_r, w_n, B, S, tm):
    BS, D = x2.shape
    nt = S // tm
    nkt = S // LANES
    row = lambda b, i: (b * nt + i, 0)
    const = lambda b, i: (0, 0)
    head_rows = lambda width: pl.BlockSpec((None, N_KV_HEADS, tm, width), lambda b, i: (b, 0, i, 0))
    head_tiles = lambda rows, width: pl.BlockSpec(
        (None, N_KV_HEADS, tm // width, rows, width), lambda b, i: (b, 0, i, 0, 0))
    kv = lambda *shape: jax.ShapeDtypeStruct((B, N_KV_HEADS) + shape, MXU_DTYPE)
    return pl.pallas_call(
        _proj_kernel,
        grid=(B, nt),
        in_specs=[
            pl.BlockSpec((tm, D), row),
            pl.BlockSpec((tm, 1), row),
            pl.BlockSpec((1, D), const),
            pl.BlockSpec((1, LANES), const),
            pl.BlockSpec((1, LANES), const),
            pl.BlockSpec((D, R_COLS_PAD), const),
            pl.BlockSpec((D, N_COLS_PAD), const),
        ],
        out_specs=[
            pl.BlockSpec((tm, R_COLS_PAD), row),
            pl.BlockSpec((tm, N_WIDTH), row),
            pl.BlockSpec((tm, 2 * KV_WIDTH), row),
            head_rows(LANES), head_tiles(HEAD + ONES_ROWS, KV_TILE),
            head_rows(HEAD), head_tiles(HEAD, LANES),
            pl.BlockSpec((None, GATE_ROWS, tm), lambda b, i: (b, 0, i)),
        ],
        out_shape=[
            jax.ShapeDtypeStruct((BS, R_COLS_PAD), F32),
            jax.ShapeDtypeStruct((BS, N_WIDTH), MXU_DTYPE),
            jax.ShapeDtypeStruct((BS, 2 * KV_WIDTH), F32),
            kv(S, LANES), kv(S // KV_TILE, HEAD + ONES_ROWS, KV_TILE),
            kv(S, HEAD), kv(nkt, HEAD, LANES),
            jax.ShapeDtypeStruct((B, GATE_ROWS, S), F32),
        ],
        compiler_params=pltpu.CompilerParams(
            dimension_semantics=("parallel", "parallel"), vmem_limit_bytes=VMEM_LIMIT),
        name="proj",
    )(x2, pos_col, nw, freq, sgn, w_r, w_n)


def _rwkv_kernel(rw_ref, mu_ref, w0_ref, w2_ref, a0_ref, a2_ref, g2_ref, kk_ref, ka_ref, rk_ref,
                 lw_ref, lb_ref, gmat_ref, ltri_ref, y_ref,
                 state_s, prev_s, at_s, rt_s, bt_s, kt_s, v_s, wc_s, y_s, bonus_s, g_s):
    B, T = rw_ref.shape[0], rw_ref.shape[1]

    @pl.when(pl.program_id(0) == 0)
    def _():
        state_s[...] = jnp.zeros_like(state_s)
        prev_s[...] = jnp.zeros_like(prev_s)

    gmat = gmat_ref[...]

    def gsum(t):
        return _dot(_mx(t), gmat)

    def prepare(b):
        rows = slice(b * T, (b + 1) * T)
        x = rw_ref[b]
        row = lax.broadcasted_iota(jnp.int32, (T, 1), 0)
        xs = jnp.where(row == 0, prev_s[b, 0:1, :], pltpu.roll(x, 1, 0))
        prev_s[b, 0:1, :] = x[T - 1:T, :]
        xm = x + (xs - x) * mu_ref[...]
        r = xm[:, 0:R_WIDTH]
        k = xm[:, R_WIDTH:2 * R_WIDTH]
        v = xm[:, 2 * R_WIDTH:3 * R_WIDTH]
        lat = xm[:, 3 * R_WIDTH:3 * R_WIDTH + LANES]
        glat = xm[:, 3 * R_WIDTH + LANES:3 * R_WIDTH + LANES + GLAT_PAD]

        wv = w0_ref[...] + _dot(_mx(jnp.tanh(lat)), w2_ref[...])
        w = -(jnp.maximum(-wv, 0.0) + jnp.log(1.0 + jnp.exp(-jnp.abs(wv)))) - 0.5
        ld = -jnp.exp(w)
        a = jax.nn.sigmoid(a0_ref[...] + _dot(_mx(lat), a2_ref[...]))
        g_s[rows, :] = _dot(_mx(jax.nn.sigmoid(glat)), g2_ref[...])
        kkv = k * kk_ref[...]
        kkn = kkv / jnp.maximum(jnp.sqrt(gsum(kkv * kkv)), 1e-12)
        k_mod = k * (1.0 + (a - 1.0) * ka_ref[...])
        bonus_s[rows, :] = gsum(r * k_mod * rk_ref[...]) * v

        ltri = ltri_ref[...]
        h1 = _mx(ld)
        r1 = ld - h1.astype(F32)
        h2 = _mx(r1)
        h3 = _mx(r1 - h2.astype(F32))
        cum = _dot(ltri, h1) + _dot(ltri, h2) + _dot(ltri, h3)
        w_inc = jnp.exp(cum)
        w_inv = jnp.exp(-cum)
        w_exc = jnp.exp(cum - ld)
        at_s[rows, :] = (-kkn * w_exc).astype(at_s.dtype)
        rt_s[rows, :] = (r * w_inc).astype(rt_s.dtype)
        bt_s[rows, :] = (kkn * a * w_inv).astype(bt_s.dtype)
        kt_s[rows, :] = (k_mod * w_inv).astype(kt_s.dtype)
        v_s[rows, :] = v
        wc_s[rows, :] = w_inc

    for b in range(B):
        prepare(b)

    rr = lax.broadcasted_iota(jnp.int32, (SLAB, SLAB), 0) // HEAD
    cc = lax.broadcasted_iota(jnp.int32, (SLAB, SLAB), 1) // HEAD
    diag_f = jnp.where(rr == cc, 1.0, 0.0)
    diag = _mx(diag_f)

    def bdiag(t):
        return jnp.concatenate([t] * (SLAB // HEAD), axis=0) * diag

    def bdiag_t(t):
        tt = jnp.concatenate([t, t], axis=0).T
        return _mx(jnp.concatenate([tt, tt], axis=1)) * diag

    ii = lax.broadcasted_iota(jnp.int32, (CHUNK, SLAB), 0)
    jj = lax.broadcasted_iota(jnp.int32, (CHUNK, SLAB), 1) % HEAD
    strict = ii > jj
    incl = ii >= jj
    eye = (ii == jj).astype(F32)
    chains = [(b, s) for b in range(B) for s in range(R_WIDTH // SLAB)]

    cs = range(len(chains))
    sl = [slice(s * SLAB, (s + 1) * SLAB) for _, s in chains]

    def state_free(c, out):
        rows = [slice(b * T + c * CHUNK, b * T + (c + 1) * CHUNK) for b, _ in chains]
        a_ = [at_s[rows[i], sl[i]] for i in cs]
        r_ = [rt_s[rows[i], sl[i]] for i in cs]
        b_ = [bt_s[rows[i], sl[i]] for i in cs]
        k_ = [kt_s[rows[i], sl[i]] for i in cs]
        vf = [v_s[rows[i], sl[i]] for i in cs]
        ar = [jnp.concatenate([a_[i], r_[i]], axis=0) for i in cs]
        out.update(rows=rows, vf=vf, ar=ar, bk=[jnp.concatenate([b_[i], k_[i]], axis=0) for i in cs])
        mb = [_dot(ar[i], bdiag_t(b_[i].astype(F32))) for i in cs]
        mk = [_dot(ar[i], bdiag_t(k_[i].astype(F32))) for i in cs]
        yield
        a_ab = [jnp.where(strict, mb[i][:CHUNK], 0.0) for i in cs]
        out["a_rb"] = [_mx(jnp.where(incl, mb[i][CHUNK:], 0.0)) for i in cs]
        a_k = [_mx(jnp.concatenate([jnp.where(strict, mk[i][:CHUNK], 0.0),
                                    jnp.where(incl, mk[i][CHUNK:], 0.0)], axis=0)) for i in cs]
        out["kv"] = [_dot(a_k[i], bdiag(_mx(vf[i]))) for i in cs]
        ab = [_mx(t) for t in a_ab]
        xp = [_dot(ab[i], bdiag(ab[i])) for i in cs]
        qs = [eye + a_ab[i] for i in cs]
        n = 2
        while n < CHUNK:
            yield
            xb = [bdiag(_mx(t)) for t in xp]
            if 2 * n < CHUNK:
                res = [_dot(_mx(jnp.concatenate([xp[i], qs[i]], axis=0)), xb[i]) for i in cs]
                xp = [res[i][:CHUNK] for i in cs]
                qs = [qs[i] + res[i][CHUNK:] for i in cs]
            else:
                upd = [_dot(_mx(qs[i]), xb[i]) for i in cs]
                qs = [qs[i] + upd[i] for i in cs]
            n *= 2
        out["inv"] = [_mx(t) for t in qs]

    def state_step(c, pre, state):
        ar, vf, rows = pre["ar"], pre["vf"], pre["rows"]
        asrs = [_dot(ar[i], bdiag_t(state[i])) for i in cs]
        yield
        rhs = [_mx(asrs[i][:CHUNK] + pre["kv"][i][:CHUNK]) for i in cs]
        u = [_dot(pre["inv"][i], bdiag(rhs[i])) for i in cs]
        yield
        yb = [_dot(pre["a_rb"][i], bdiag(_mx(u[i]))) for i in cs]
        uv_t = [_mx(jnp.concatenate([u[i], vf[i]], axis=0).T) for i in cs]
        full = [_dot(uv_t[i], pre["bk"][i]) * diag_f for i in cs]
        for i in cs:
            ds = full[i][0:HEAD]
            for h in range(1, SLAB // HEAD):
                ds = ds + full[i][h * HEAD:(h + 1) * HEAD]
            wc = wc_s[rows[i].stop - 1:rows[i].stop, sl[i]]
            state[i] = (state[i] + ds) * wc
            y_s[rows[i], sl[i]] = asrs[i][CHUNK:] + yb[i] + pre["kv"][i][CHUNK:]

    n_chunks = T // CHUNK
    state = [state_s[i] for i in cs]
    pre = {}
    for _ in state_free(0, pre):
        pass
    for c in range(n_chunks):
        nxt = {}
        streams = [state_step(c, pre, state)]
        if c + 1 < n_chunks:
            streams.append(state_free(c + 1, nxt))
        while streams:
            streams = [g for g in streams if next(g, streams) is not streams]
        pre = nxt
    for i in cs:
        state_s[i] = state[i]

    for b in range(B):
        rows = slice(b * T, (b + 1) * T)
        y = y_s[rows, :]
        mean = gsum(y) * (1.0 / HEAD)
        d = y - mean
        var = gsum(d * d) * (1.0 / HEAD)
        yn = d * lax.rsqrt(var + LNX_EPS) * lw_ref[...] + lb_ref[...]
        y_ref[b] = ((yn + bonus_s[rows, :]) * g_s[rows, :]).astype(y_ref.dtype)


def _rwkv(rw, mu, w0, w2p, a0, a2p, g2p, k_k, k_a, r_k, lnx_w, lnx_b, gmat, ltri, B, S, T):
    const = lambda i: (0, 0)
    vec = pl.BlockSpec((1, R_WIDTH), const)
    big = lambda dt: pltpu.VMEM((B * T, R_WIDTH), dt)
    y = pl.pallas_call(
        _rwkv_kernel,
        grid=(S // T,),
        in_specs=[
            pl.BlockSpec((B, T, R_COLS_PAD), lambda i: (0, i, 0)),
            pl.BlockSpec((1, R_COLS_PAD), const),
            vec,
            pl.BlockSpec((LANES, R_WIDTH), const),
            vec,
            pl.BlockSpec((LANES, R_WIDTH), const),
            pl.BlockSpec((GLAT_PAD, R_WIDTH), const),
            vec, vec, vec, vec, vec,
            pl.BlockSpec((R_WIDTH, R_WIDTH), const),
            pl.BlockSpec((T, T), const),
        ],
        out_specs=pl.BlockSpec((B, T, R_WIDTH), lambda i: (0, i, 0)),
        out_shape=jax.ShapeDtypeStruct((B, S, R_WIDTH), MXU_DTYPE),
        scratch_shapes=[
            pltpu.VMEM((B * (R_WIDTH // SLAB), HEAD, SLAB), F32),
            pltpu.VMEM((B, 8, R_COLS_PAD), F32),
            big(MXU_DTYPE), big(MXU_DTYPE), big(MXU_DTYPE), big(MXU_DTYPE),
            big(F32), big(F32), big(F32), big(F32), big(F32),
        ],
        compiler_params=pltpu.CompilerParams(
            dimension_semantics=("arbitrary",), vmem_limit_bytes=VMEM_LIMIT),
        name="rwkv",
    )(rw.reshape(B, S, R_COLS_PAD), mu, w0, w2p, a0, a2p, g2p, k_k, k_a, r_k, lnx_w, lnx_b, gmat, ltri)
    return y.reshape(B * S, R_WIDTH)


def _cmp_kernel(x_ref, w1_ref, b1_ref, w2_ref, pe_ref, pos_ref, freq_ref, sgn_ref, o_ref, ot_ref):
    n_blk = x_ref.shape[0]
    half = x_ref.shape[1]
    j = pl.program_id(1)
    xb = _mx(x_ref[...])
    w1 = w1_ref[...]
    first_half = _dot(xb, w1[:half])
    second_half = pltpu.roll(_dot(xb, w1[half:]), n_blk - 1, 0)
    bias = _dot(_mx(pe_ref[...]), w1)[0:1] + b1_ref[...]
    hdn = first_half + second_half + bias
    hdn = hdn * jax.nn.sigmoid(hdn)
    out = _dot(_mx(hdn), w2_ref[...])
    cos, sin = _rope_tables(pos_ref[...], freq_ref[...], sgn_ref[...])
    lane = lax.broadcasted_iota(jnp.int32, (1, LANES), 1)
    roped = _rope128(out, cos, sin, (lane % HEAD) < ROT_HALF)
    res = jnp.where(j < N_KV_HEADS, roped, out)
    row = lax.broadcasted_iota(jnp.int32, (n_blk, 1), 0)
    res = jnp.where(row == n_blk - 1, 0.0, res)
    o_ref[...] = res[:, :HEAD].astype(o_ref.dtype)
    ot_ref[...] = res.T[:HEAD].astype(ot_ref.dtype)


def _compress(xblk, w1s, b1s, w2s, pes, pos_c, freq, sgn, B, n_blk):
    half = xblk.shape[-1]
    return pl.pallas_call(
        _cmp_kernel,
        grid=(B, 2 * N_KV_HEADS),
        in_specs=[
            pl.BlockSpec((None, None, n_blk, half), lambda b, j: (b, j, 0, 0)),
            pl.BlockSpec((None, 2 * half, CMP_HIDDEN), lambda b, j: (j // N_KV_HEADS, 0, 0)),
            pl.BlockSpec((None, 1, CMP_HIDDEN), lambda b, j: (j // N_KV_HEADS, 0, 0)),
            pl.BlockSpec((None, CMP_HIDDEN, LANES), lambda b, j: (j // N_KV_HEADS, 0, 0)),
            pl.BlockSpec((None, 8, 2 * half), lambda b, j: (j // N_KV_HEADS, 0, 0)),
            pl.BlockSpec((None, n_blk, 1), lambda b, j: (b, 0, 0)),
            pl.BlockSpec((1, LANES), lambda b, j: (0, 0)),
            pl.BlockSpec((1, LANES), lambda b, j: (0, 0)),
        ],
        out_specs=[
            pl.BlockSpec((None, None, n_blk, HEAD), lambda b, j: (b, j, 0, 0)),
            pl.BlockSpec((None, None, HEAD, n_blk), lambda b, j: (b, j, 0, 0)),
        ],
        out_shape=[
            jax.ShapeDtypeStruct((B, 2 * N_KV_HEADS, n_blk, HEAD), MXU_DTYPE),
            jax.ShapeDtypeStruct((B, 2 * N_KV_HEADS, HEAD, n_blk), MXU_DTYPE),
        ],
        compiler_params=pltpu.CompilerParams(
            dimension_semantics=("parallel", "parallel"), vmem_limit_bytes=VMEM_LIMIT),
        name="compress",
    )(xblk, w1s, b1s, w2s, pes, pos_c, freq, sgn)


def _nsa_kernel(q_ref, kc_ref, vct_ref, ks_ref, vst_ref, kw_ref, vwt_ref, g_ref, o_ref,
                sel_s, ocmp_s, owin_s, sbuf_s, *, n_top, levels):
    n_blk = kc_ref.shape[0]
    n_sel = sel_s.shape[0]
    hk = pl.program_id(1)
    qb = pl.program_id(2)
    start = qb * Q_BLOCK
    W = GQA * Q_BLOCK

    q_t = q_ref[...].astype(F32).T
    qt = _mx(jnp.concatenate([q_t[g * HEAD:(g + 1) * HEAD] for g in range(GQA)], axis=1))
    tile4 = lambda t: jnp.concatenate([t] * GQA, axis=1)
    qi = lax.broadcasted_iota(jnp.int32, (1, Q_BLOCK), 1)
    t_q = start + qi
    t_w = tile4(t_q)

    def compressed_and_select(n_c, n_s):
        w0 = pl.multiple_of(jnp.maximum(start - WINDOW, 0), Q_BLOCK)
        n_win = WINDOW + Q_BLOCK
        s_win = _dot(kw_ref[pl.ds(w0, n_win), :], qt)

        c_end = lax.broadcasted_iota(jnp.int32, (n_c, 1), 0) * CMP_STRIDE + (CMP_BLOCK - 1)
        sc = _dot(kc_ref[0:n_c, :], qt) + tile4(jnp.where(c_end <= t_q, 0.0, MASK))

        tok = w0 + lax.broadcasted_iota(jnp.int32, (n_win, 1), 0)
        in_win = (tok <= t_q) & (tok > t_q - WINDOW)
        s_win = s_win + tile4(jnp.where(in_win, 0.0, MASK))
        p_win = jnp.exp2(s_win - jnp.max(s_win, axis=0, keepdims=True))
        l_win = jnp.sum(p_win, axis=0, keepdims=True)
        p_win = _mx(p_win)
        acc_win = jnp.zeros((HEAD, W), F32)
        for i in range(n_win // Q_BLOCK):
            acc_win = acc_win + _dot(vwt_ref[w0 // Q_BLOCK + i], p_win[i * Q_BLOCK:(i + 1) * Q_BLOCK])
        owin_s[...] = acc_win * (1.0 / l_win)

        m = jnp.max(sc, axis=0, keepdims=True)
        p = jnp.exp2(sc - m)
        l = jnp.sum(p, axis=0, keepdims=True)
        pn = p * jnp.where(t_w >= CMP_BLOCK - 1, 1.0 / l, 0.0)
        ocmp_s[...] = _dot(vct_ref[:, 0:n_c], _mx(pn))

        ps = pn[:, 0:Q_BLOCK]
        for g in range(1, GQA):
            ps = ps + pn[:, g * Q_BLOCK:(g + 1) * Q_BLOCK]
        hi = _mx(ps)
        lo = _mx(ps - hi.astype(F32))
        j_col = lax.broadcasted_iota(jnp.int32, (n_s, 1), 0)
        c_row = lax.broadcasted_iota(jnp.int32, (1, n_c), 1)
        cover = ((c_row * CMP_STRIDE < j_col * SEL_BLOCK + SEL_BLOCK)
                 & (c_row * CMP_STRIDE + CMP_BLOCK > j_col * SEL_BLOCK))
        cover = jnp.where(cover, 1.0, 0.0).astype(MXU_DTYPE)
        imp = _dot(cover, hi) + _dot(cover, lo)

        cur = t_q // SEL_BLOCK
        forced = (j_col == 0) | (j_col == cur) | (j_col == cur - 1)
        score = jnp.where(forced, REMOVED, jnp.where(j_col <= cur, imp, MASK))
        sel = jnp.where(forced, 1.0, 0.0)
        for _ in range(max(n_top - 3, 0)):
            mx = jnp.max(score, axis=0, keepdims=True)
            idx = jnp.min(jnp.where(score == mx, j_col, n_s), axis=0, keepdims=True)
            pick = j_col == idx
            sel = jnp.where(pick, 1.0, sel)
            score = jnp.where(pick, REMOVED, score)
        sel_s[0:n_s, :] = sel

    lo_qb = 0
    for hi_qb in levels:
        n_c = min(hi_qb * (Q_BLOCK // CMP_STRIDE), n_blk)
        n_s = min(hi_qb * (Q_BLOCK // SEL_BLOCK), n_sel)
        pl.when((qb >= lo_qb) & (qb < hi_qb))(functools.partial(compressed_and_select, n_c, n_s))
        lo_qb = hi_qb
    o_cmp = ocmp_s[...]

    tiles = GROUP // KV_TILE
    zero_rows = jnp.zeros((LANES - HEAD - 2 * SEL_GROUP, W), MXU_DTYPE)

    def group_scores(j):
        grp = sel_s[pl.ds(pl.multiple_of(j * SEL_GROUP, SEL_GROUP), SEL_GROUP), :]
        neg = jnp.where(grp > 0.5, 0.0, MASK)
        neg = tile4(jnp.concatenate([neg, jnp.zeros_like(neg)], axis=0))
        q_aug = jnp.concatenate([qt, _mx(neg), zero_rows], axis=0)
        r0 = pl.multiple_of(j * GROUP, GROUP)
        return [_dot(ks_ref[pl.ds(r0 + i * KV_TILE, KV_TILE), :], q_aug) for i in range(tiles)]

    def put_scores(slot, j):
        for i, s in enumerate(group_scores(j)):
            sbuf_s[slot, i * KV_TILE:(i + 1) * KV_TILE, :] = s

    def online(s, vt_tile, carry):
        m_, acc = carry
        m_new = jnp.maximum(m_, jnp.max(s, axis=0, keepdims=True))
        alpha = jnp.exp2(m_ - m_new)
        return m_new, alpha * acc + _dot(vt_tile, _mx(jnp.exp2(s - m_new)))

    def consume(slot, j, carry):
        for i in range(tiles):
            s = sbuf_s[slot, i * KV_TILE:(i + 1) * KV_TILE, :]
            carry = online(s, vst_ref[j * tiles + i], carry)
        return carry

    def pair_body(jp, carry):
        j = 2 * jp
        put_scores(1, j + 1)
        carry = consume(0, j, carry)
        put_scores(0, j + 2)
        return consume(1, j + 1, carry)

    def single_body(j, carry):
        carry = consume(0, j, carry)
        put_scores(0, j + 1)
        return carry

    init = (jnp.full((1, W), MASK, F32), jnp.zeros((HEAD + ONES_ROWS, W), F32))
    last = qb // (GROUP // Q_BLOCK)
    put_scores(0, 0)
    quad_body = lambda jq, c: pair_body(2 * jq + 1, pair_body(2 * jq, c))
    carry = lax.fori_loop(0, last // 4, quad_body, init)
    carry = lax.fori_loop(2 * (last // 4), last // 2, pair_body, carry)
    carry = lax.fori_loop(2 * (last // 2), last, single_body, carry)

    k_row = lax.broadcasted_iota(jnp.int32, (KV_TILE, 1), 0)
    for i in range(tiles):
        tok = last * GROUP + i * KV_TILE + k_row
        causal = tile4(jnp.where(tok <= t_q, 0.0, MASK))
        s = sbuf_s[0, i * KV_TILE:(i + 1) * KV_TILE, :] + causal
        carry = online(s, vst_ref[last * tiles + i], carry)
    _, acc_sel = carry
    o_sel = acc_sel[:HEAD] * (1.0 / acc_sel[HEAD:HEAD + 1])
    o_win = owin_s[...]

    outs = []
    gates = g_ref[...]
    per_kv = GQA * 3
    gate = lambda r: jnp.where(hk == 0, gates[r:r + 1], gates[per_kv + r:per_kv + r + 1])
    for g in range(GQA):
        cols = slice(g * Q_BLOCK, (g + 1) * Q_BLOCK)
        o_g = (gate(g * 3) * o_cmp[:, cols] + gate(g * 3 + 1) * o_sel[:, cols]
               + gate(g * 3 + 2) * o_win[:, cols])
        outs.append(o_g)
    o_ref[...] = jnp.concatenate(outs, axis=0).T.astype(o_ref.dtype)


def _nsa(q, kc, vct, ks, vst, kw, vwt, gt, B, S):
    BS = q.shape[0]
    nqb = S // Q_BLOCK
    n_blk = S // CMP_STRIDE
    n_sel = S // SEL_BLOCK
    n_top = min(SEL_TOPK, n_sel)
    levels = tuple(b for b in (nqb // 4, nqb // 2) if b > 0) + (nqb,)
    per_head = lambda *blk: pl.BlockSpec((None, None) + blk, lambda b, h, i: (b, h) + (0,) * len(blk))
    qspec = pl.BlockSpec((Q_BLOCK, GQA * HEAD), lambda b, h, i: (b * nqb + i, h))
    return pl.pallas_call(
        functools.partial(_nsa_kernel, n_top=n_top, levels=levels),
        grid=(B, N_KV_HEADS, nqb),
        in_specs=[
            qspec,
            per_head(n_blk, HEAD),
            per_head(HEAD, n_blk),
            per_head(S, LANES),
            per_head(S // KV_TILE, HEAD + ONES_ROWS, KV_TILE),
            per_head(S, HEAD),
            per_head(nqb, HEAD, LANES),
            pl.BlockSpec((None, GATE_ROWS, Q_BLOCK), lambda b, h, i: (b, 0, i)),
        ],
        out_specs=qspec,
        out_shape=jax.ShapeDtypeStruct((BS, N_WIDTH), MXU_DTYPE),
        scratch_shapes=[
            pltpu.VMEM((n_sel, Q_BLOCK), F32),
            pltpu.VMEM((HEAD, GQA * Q_BLOCK), F32),
            pltpu.VMEM((HEAD, GQA * Q_BLOCK), F32),
            pltpu.VMEM((2, GROUP, GQA * Q_BLOCK), F32),
        ],
        compiler_params=pltpu.CompilerParams(
            dimension_semantics=("parallel", "parallel", "arbitrary"), vmem_limit_bytes=VMEM_LIMIT),
        name="nsa",
    )(q, kc, vct, ks, vst, kw, vwt, gt)


def _ffn_kernel(x_ref, yr_ref, yn_ref, wo_ref, nf_ref, wg_ref, wu_ref, wd_ref, nfin_ref, o_ref,
                h1_s, xn_s, acc_s):
    f = pl.program_id(1)

    @pl.when(f == 0)
    def _():
        mix = _dot(yr_ref[...], wo_ref[:R_WIDTH]) + _dot(yn_ref[...], wo_ref[R_WIDTH:])
        h1 = x_ref[...] + mix
        h1_s[...] = h1
        xn_s[...] = _rms(h1, nf_ref[...]).astype(xn_s.dtype)
        acc_s[...] = jnp.zeros_like(acc_s)

    xn = xn_s[...]
    gate = _dot(xn, wg_ref[...])
    up = _dot(xn, wu_ref[...])
    acc_s[...] += _dot(_mx(gate * jax.nn.sigmoid(gate) * up), wd_ref[...])

    @pl.when(f == pl.num_programs(1) - 1)
    def _():
        o_ref[...] = _rms(h1_s[...] + acc_s[...], nfin_ref[...])


def _ffn(x2, y_r, y_n, w_out, nf, w_gate, w_up, w_down, nfin, tm, tf):
    BS, D = x2.shape
    d_ff = w_gate.shape[1]
    row = lambda i, f: (i, 0)
    const = lambda i, f: (0, 0)
    return pl.pallas_call(
        _ffn_kernel,
        grid=(BS // tm, d_ff // tf),
        in_specs=[
            pl.BlockSpec((tm, D), row),
            pl.BlockSpec((tm, R_WIDTH), row),
            pl.BlockSpec((tm, N_WIDTH), row),
            pl.BlockSpec((R_WIDTH + N_WIDTH, D), const),
            pl.BlockSpec((1, D), const),
            pl.BlockSpec((D, tf), lambda i, f: (0, f)),
            pl.BlockSpec((D, tf), lambda i, f: (0, f)),
            pl.BlockSpec((tf, D), lambda i, f: (f, 0)),
            pl.BlockSpec((1, D), const),
        ],
        out_specs=pl.BlockSpec((tm, D), row),
        out_shape=jax.ShapeDtypeStruct((BS, D), F32),
        scratch_shapes=[pltpu.VMEM((tm, D), F32), pltpu.VMEM((tm, D), MXU_DTYPE), pltpu.VMEM((tm, D), F32)],
        compiler_params=pltpu.CompilerParams(
            dimension_semantics=("parallel", "arbitrary"), vmem_limit_bytes=VMEM_LIMIT),
        name="ffn",
    )(x2, y_r, y_n, w_out, nf, w_gate, w_up, w_down, nfin)


def _pad_to(a, n, axis):
    pad = [(0, 0)] * a.ndim
    pad[axis] = (0, n - a.shape[axis])
    return jnp.pad(a, pad)


def _rope_rows():
    lane = jnp.arange(LANES) % HEAD
    inv_freq = jnp.power(jnp.float32(ROPE_THETA), -jnp.arange(ROT_HALF, dtype=F32) * 2.0 / ROT_DIM)
    freq = jnp.where(lane < ROT_DIM, inv_freq[lane % ROT_HALF], 0.0).astype(F32)[None, :]
    sgn = jnp.where(lane < ROT_HALF, -1.0, jnp.where(lane < ROT_DIM, 1.0, 0.0)).astype(F32)[None, :]
    return freq, sgn


def _layer(h, positions, norm_mix, w_in, mu_shift, w0, w2, a0, a2, g2, k_k, k_a, r_k, lnx_w, lnx_b,
           pe_k, wk1, bk1, wk2, pe_v, wv1, bv1, wv2, w_out, norm_ffn, w_gate, w_up, w_down, norm_final):
    B, S, D = h.shape
    BS = B * S
    x2 = h.reshape(BS, D)
    freq, sgn = _rope_rows()
    row = lambda a: a.reshape(1, -1).astype(F32)

    w_r = _mx(_pad_to(w_in[:, :R_COLS], R_COLS_PAD, 1))
    w_n = _mx(_pad_to(w_in[:, R_COLS:], N_COLS_PAD, 1))
    tm = min(256, S)
    rw, q, kvc, ks, vst, kw, vwt, gt = _proj(
        x2, positions.reshape(BS, 1), row(norm_mix), freq, sgn, w_r, w_n, B, S, tm)

    T = min(256, S)
    head_of = jnp.arange(R_WIDTH) // HEAD
    gmat = (head_of[:, None] == head_of[None, :]).astype(MXU_DTYPE)
    tt = jnp.arange(T)
    ltri = ((tt[:, None] // CHUNK == tt[None, :] // CHUNK) & (tt[:, None] >= tt[None, :])).astype(MXU_DTYPE)
    w2p = _mx(_pad_to(w2, LANES, 0))
    a2p = _mx(jnp.concatenate([jnp.zeros((DECAY_LORA, R_WIDTH), F32), a2], axis=0))
    g2p = _mx(_pad_to(g2, GLAT_PAD, 0))
    y_r = _rwkv(rw, row(_pad_to(mu_shift, R_COLS_PAD, 0)), row(w0), w2p, row(a0), a2p, g2p,
                row(k_k), row(k_a), row(r_k), row(lnx_w), row(lnx_b), gmat, ltri, B, S, T)

    n_blk = S // CMP_STRIDE
    xblk = kvc.reshape(B, n_blk, CMP_STRIDE, 2 * N_KV_HEADS, HEAD).transpose(0, 3, 1, 2, 4)
    xblk = xblk.reshape(B, 2 * N_KV_HEADS, n_blk, CMP_STRIDE * HEAD)
    w1s = _mx(jnp.stack([wk1, wv1]))
    b1s = jnp.stack([bk1, bv1]).reshape(2, 1, CMP_HIDDEN).astype(F32)
    w2s = _mx(_pad_to(jnp.stack([wk2, wv2]), LANES, 2))
    pes = _pad_to(jnp.stack([pe_k, pe_v]).reshape(2, 1, CMP_BLOCK * HEAD), 8, 1).astype(F32)
    pos_c = _pad_to(positions[:, CMP_BLOCK - 1::CMP_STRIDE], n_blk, 1).reshape(B, n_blk, 1)
    cmp_rows, cmp_t = _compress(xblk, w1s, b1s, w2s, pes, pos_c, freq, sgn, B, n_blk)
    kc = cmp_rows[:, :N_KV_HEADS]
    vct = cmp_t[:, N_KV_HEADS:]

    y_n = _nsa(q, kc, vct, ks, vst, kw, vwt, gt, B, S)

    d_ff = w_gate.shape[1]
    tf = d_ff // 2 if (d_ff // 2) % LANES == 0 else d_ff
    out = _ffn(x2, y_r, y_n, _mx(w_out), row(norm_ffn), _mx(w_gate), _mx(w_up), _mx(w_down),
               row(norm_final), min(512, BS), tf)
    return out.reshape(B, S, D)


def kernel(x, positions, norm_mix, w_in, mu_shift, w0, w2, a0, a2, g2, k_k, k_a, r_k, lnx_w, lnx_b, pe_k, wk1, bk1, wk2, pe_v, wv1, bv1, wv2, w_out, norm_ffn, w_gate, w_up, w_down, norm_final):
    assert norm_mix.shape[0] == 1, "single-layer block"
    l = 0
    return _layer(x, positions, norm_mix[l], w_in[l], mu_shift[l], w0[l], w2[l], a0[l], a2[l], g2[l],
                  k_k[l], k_a[l], r_k[l].reshape(-1), lnx_w[l], lnx_b[l], pe_k[l], wk1[l], bk1[l], wk2[l],
                  pe_v[l], wv1[l], bv1[l], wv2[l], w_out[l], norm_ffn[l], w_gate[l], w_up[l], w_down[l],
                  norm_final)
```

```python
import functools

import jax
import jax.numpy as jnp
from jax import lax
from jax.experimental import pallas as pl
from jax.experimental.pallas import tpu as pltpu

F32 = jnp.float32
MXU_DTYPE = jnp.bfloat16

R_HEADS = 8
HEAD = 64
R_WIDTH = R_HEADS * HEAD
DECAY_LORA = 64
ICLR_LORA = 64
GATE_LORA = 160
LNX_EPS = 64e-5
R_COLS = 3 * R_WIDTH + DECAY_LORA + ICLR_LORA + GATE_LORA
N_Q_HEADS = 8
N_KV_HEADS = 2
GQA = N_Q_HEADS // N_KV_HEADS
N_WIDTH = N_Q_HEADS * HEAD
KV_WIDTH = N_KV_HEADS * HEAD
CMP_BLOCK = 32
CMP_STRIDE = 16
CMP_HIDDEN = 256
SEL_BLOCK = 64
SEL_TOPK = 16
WINDOW = 512
Q_BLOCK = 128
ROPE_THETA = 500000.0
ROT_DIM = HEAD // 4
ROT_HALF = ROT_DIM // 2
NORM_EPS = 1e-6
MASK = -1e30
FORCE = 1e6
REMOVED = -(2.0 ** 127)

LANES = 128
R_COLS_PAD = 1920
GLAT_PAD = 256
N_COLS_PAD = N_WIDTH + 6 * KV_WIDTH + LANES
GATE_ROWS = 32
CHUNK = 64
SLAB = 4 * HEAD
SEL_GROUP = 8
GROUP = SEL_GROUP * SEL_BLOCK
KV_TILE = 256
ONES_ROWS = 16
LOG2E = 1.4426950408889634
VMEM_LIMIT = 56 * 1024 * 1024


def _dot(a, b):
    return jnp.dot(a, b, preferred_element_type=F32)


def _dot_nt(a, b):
    return lax.dot_general(a, b, (((1,), (1,)), ((), ())), preferred_element_type=F32)


def _mx(a):
    return a.astype(MXU_DTYPE)


def _rms(x, w):
    ms = jnp.mean(x * x, axis=-1, keepdims=True)
    return x * lax.rsqrt(ms + NORM_EPS) * w


def _rope_tables(pos_col, freq_row, sgn_row):
    ang = pos_col.astype(F32) * freq_row
    return jnp.cos(ang), jnp.sin(ang) * sgn_row


def _rope128(t, cos, sin, first):
    sh = jnp.where(first, pltpu.roll(t, LANES - ROT_HALF, 1), pltpu.roll(t, ROT_HALF, 1))
    return t * cos + sh * sin


def _proj_kernel(x_ref, pos_ref, nw_ref, freq_ref, sgn_ref, wr_ref, wn_ref,
                 rw_ref, q_ref, kvc_ref, ks_ref, vst_ref, kw_ref, vwt_ref, gt_ref):
    tm = x_ref.shape[0]
    xn = _mx(_rms(x_ref[...], nw_ref[...]))
    rw_ref[...] = _dot(xn, wr_ref[...])
    pn = _dot(xn, wn_ref[...])
    cos, sin = _rope_tables(pos_ref[...], freq_ref[...], sgn_ref[...])
    lane = lax.broadcasted_iota(jnp.int32, (1, LANES), 1)
    first = (lane % HEAD) < ROT_HALF
    rope = lambda t: _rope128(t, cos, sin, first)
    scale = HEAD ** -0.5 * LOG2E
    q = [rope(pn[:, i * LANES:(i + 1) * LANES]) * scale for i in range(N_WIDTH // LANES)]
    q_ref[...] = jnp.concatenate(q, axis=1).astype(q_ref.dtype)
    o = N_WIDTH
    kvc_ref[...] = pn[:, o:o + 2 * KV_WIDTH]
    o += 2 * KV_WIDTH

    row_g = pl.program_id(1) * tm + lax.broadcasted_iota(jnp.int32, (tm, 1), 0)
    onehot = jnp.where(lane == HEAD + (row_g // SEL_BLOCK) % SEL_GROUP, 1.0, 0.0)
    kk = rope(pn[:, o:o + KV_WIDTH])
    vt = rope(pn[:, o + KV_WIDTH:o + 2 * KV_WIDTH]).T
    o += 2 * KV_WIDTH
    ones_rows = jnp.where(lax.broadcasted_iota(jnp.int32, (ONES_ROWS, KV_TILE), 0) == 0, 1.0, 0.0)
    for h in range(N_KV_HEADS):
        k_h = kk if h == 0 else pltpu.roll(kk, HEAD, 1)
        ks_ref[h] = jnp.where(lane < HEAD, k_h, onehot).astype(ks_ref.dtype)
        for i in range(tm // KV_TILE):
            v_t = vt[h * HEAD:(h + 1) * HEAD, i * KV_TILE:(i + 1) * KV_TILE]
            vst_ref[h, i] = jnp.concatenate([v_t, ones_rows], axis=0).astype(vst_ref.dtype)

    kk = rope(pn[:, o:o + KV_WIDTH])
    vt = rope(pn[:, o + KV_WIDTH:o + 2 * KV_WIDTH]).T
    o += 2 * KV_WIDTH
    for h in range(N_KV_HEADS):
        kw_ref[h] = kk[:, h * HEAD:(h + 1) * HEAD].astype(kw_ref.dtype)
        for i in range(tm // LANES):
            vwt_ref[h, i] = vt[h * HEAD:(h + 1) * HEAD, i * LANES:(i + 1) * LANES].astype(vwt_ref.dtype)
    gt_ref[...] = jax.nn.sigmoid(pn[:, o:o + LANES]).T[:GATE_ROWS]


def _proj(x2, pos_col, nw, freq, sgn, w_r, w_n, B, S, tm):
    BS, D = x2.shape
    nt = S // tm
    nkt = S // LANES
    row = lambda b, i: (b * nt + i, 0)
    const = lambda b, i: (0, 0)
    head_rows = lambda width: pl.BlockSpec((None, N_KV_HEADS, tm, width), lambda b, i: (b, 0, i, 0))
    head_tiles = lambda rows, width: pl.BlockSpec(
        (None, N_KV_HEADS, tm // width, rows, width), lambda b, i: (b, 0, i, 0, 0))
    kv = lambda *shape: jax.ShapeDtypeStruct((B, N_KV_HEADS) + shape, MXU_DTYPE)
    return pl.pallas_call(
        _proj_kernel,
        grid=(B, nt),
        in_specs=[
            pl.BlockSpec((tm, D), row),
            pl.BlockSpec((tm, 1), row),
            pl.BlockSpec((1, D), const),
            pl.BlockSpec((1, LANES), const),
            pl.BlockSpec((1, LANES), const),
            pl.BlockSpec((D, R_COLS_PAD), const),
            pl.BlockSpec((D, N_COLS_PAD), const),
        ],
        out_specs=[
            pl.BlockSpec((tm, R_COLS_PAD), row),
            pl.BlockSpec((tm, N_WIDTH), row),
            pl.BlockSpec((tm, 2 * KV_WIDTH), row),
            head_rows(LANES), head_tiles(HEAD + ONES_ROWS, KV_TILE),
            head_rows(HEAD), head_tiles(HEAD, LANES),
            pl.BlockSpec((None, GATE_ROWS, tm), lambda b, i: (b, 0, i)),
        ],
        out_shape=[
            jax.ShapeDtypeStruct((BS, R_COLS_PAD), F32),
            jax.ShapeDtypeStruct((BS, N_WIDTH), MXU_DTYPE),
            jax.ShapeDtypeStruct((BS, 2 * KV_WIDTH), F32),
            kv(S, LANES), kv(S // KV_TILE, HEAD + ONES_ROWS, KV_TILE),
            kv(S, HEAD), kv(nkt, HEAD, LANES),
            jax.ShapeDtypeStruct((B, GATE_ROWS, S), F32),
        ],
        compiler_params=pltpu.CompilerParams(
            dimension_semantics=("parallel", "parallel"), vmem_limit_bytes=VMEM_LIMIT),
        name="proj",
    )(x2, pos_col, nw, freq, sgn, w_r, w_n)


def _rwkv_kernel(rw_ref, mu_ref, w0_ref, w2_ref, a0_ref, a2_ref, g2_ref, kk_ref, ka_ref, rk_ref,
                 lw_ref, lb_ref, gmat_ref, ltri_ref, y_ref,
                 state_s, prev_s, at_s, rt_s, bt_s, kt_s, v_s, wc_s, y_s, bonus_s, g_s):
    B, T = rw_ref.shape[0], rw_ref.shape[1]

    @pl.when(pl.program_id(0) == 0)
    def _():
        state_s[...] = jnp.zeros_like(state_s)
        prev_s[...] = jnp.zeros_like(prev_s)

    gmat = gmat_ref[...]

    def gsum(t):
        return _dot(_mx(t), gmat)

    def prepare(b):
        rows = slice(b * T, (b + 1) * T)
        x = rw_ref[b]
        row = lax.broadcasted_iota(jnp.int32, (T, 1), 0)
        xs = jnp.where(row == 0, prev_s[b, 0:1, :], pltpu.roll(x, 1, 0))
        prev_s[b, 0:1, :] = x[T - 1:T, :]
        xm = x + (xs - x) * mu_ref[...]
        r = xm[:, 0:R_WIDTH]
        k = xm[:, R_WIDTH:2 * R_WIDTH]
        v = xm[:, 2 * R_WIDTH:3 * R_WIDTH]
        lat = xm[:, 3 * R_WIDTH:3 * R_WIDTH + LANES]
        glat = xm[:, 3 * R_WIDTH + LANES:3 * R_WIDTH + LANES + GLAT_PAD]

        wv = w0_ref[...] + _dot(_mx(jnp.tanh(lat)), w2_ref[...])
        w = -(jnp.maximum(-wv, 0.0) + jnp.log(1.0 + jnp.exp(-jnp.abs(wv)))) - 0.5
        ld = -jnp.exp(w)
        a = jax.nn.sigmoid(a0_ref[...] + _dot(_mx(lat), a2_ref[...]))
        g_s[rows, :] = _dot(_mx(jax.nn.sigmoid(glat)), g2_ref[...])
        kkv = k * kk_ref[...]
        kkn = kkv / jnp.maximum(jnp.sqrt(gsum(kkv * kkv)), 1e-12)
        k_mod = k * (1.0 + (a - 1.0) * ka_ref[...])
        bonus_s[rows, :] = gsum(r * k_mod * rk_ref[...]) * v

        ltri = ltri_ref[...]
        h1 = _mx(ld)
        r1 = ld - h1.astype(F32)
        h2 = _mx(r1)
        h3 = _mx(r1 - h2.astype(F32))
        cum = _dot(ltri, h1) + _dot(ltri, h2) + _dot(ltri, h3)
        w_inc = jnp.exp(cum)
        w_inv = jnp.exp(-cum)
        w_exc = jnp.exp(cum - ld)
        at_s[rows, :] = (-kkn * w_exc).astype(at_s.dtype)
        rt_s[rows, :] = (r * w_inc).astype(rt_s.dtype)
        bt_s[rows, :] = (kkn * a * w_inv).astype(bt_s.dtype)
        kt_s[rows, :] = (k_mod * w_inv).astype(kt_s.dtype)
        v_s[rows, :] = v
        wc_s[rows, :] = w_inc

    for b in range(B):
        prepare(b)

    rr = lax.broadcasted_iota(jnp.int32, (SLAB, SLAB), 0) // HEAD
    cc = lax.broadcasted_iota(jnp.int32, (SLAB, SLAB), 1) // HEAD
    diag_f = jnp.where(rr == cc, 1.0, 0.0)
    diag = _mx(diag_f)

    def bdiag(t):
        return jnp.concatenate([t] * (SLAB // HEAD), axis=0) * diag

    def bdiag_t(t):
        tt = jnp.concatenate([t, t], axis=0).T
        return _mx(jnp.concatenate([tt, tt], axis=1)) * diag

    ii = lax.broadcasted_iota(jnp.int32, (CHUNK, SLAB), 0)
    jj = lax.broadcasted_iota(jnp.int32, (CHUNK, SLAB), 1) % HEAD
    strict = ii > jj
    incl = ii >= jj
    eye = (ii == jj).astype(F32)
    chains = [(b, s) for b in range(B) for s in range(R_WIDTH // SLAB)]

    cs = range(len(chains))
    sl = [slice(s * SLAB, (s + 1) * SLAB) for _, s in chains]

    def state_free(c, out):
        rows = [slice(b * T + c * CHUNK, b * T + (c + 1) * CHUNK) for b, _ in chains]
        a_ = [at_s[rows[i], sl[i]] for i in cs]
        r_ = [rt_s[rows[i], sl[i]] for i in cs]
        b_ = [bt_s[rows[i], sl[i]] for i in cs]
        k_ = [kt_s[rows[i], sl[i]] for i in cs]
        vf = [v_s[rows[i], sl[i]] for i in cs]
        ar = [jnp.concatenate([a_[i], r_[i]], axis=0) for i in cs]
        out.update(rows=rows, vf=vf, ar=ar, bk=[jnp.concatenate([b_[i], k_[i]], axis=0) for i in cs])
        mb = [_dot(ar[i], bdiag_t(b_[i].astype(F32))) for i in cs]
        mk = [_dot(ar[i], bdiag_t(k_[i].astype(F32))) for i in cs]
        yield
        a_ab = [jnp.where(strict, mb[i][:CHUNK], 0.0) for i in cs]
        out["a_rb"] = [_mx(jnp.where(incl, mb[i][CHUNK:], 0.0)) for i in cs]
        a_k = [_mx(jnp.concatenate([jnp.where(strict, mk[i][:CHUNK], 0.0),
                                    jnp.where(incl, mk[i][CHUNK:], 0.0)], axis=0)) for i in cs]
        out["kv"] = [_dot(a_k[i], bdiag(_mx(vf[i]))) for i in cs]
        ab = [_mx(t) for t in a_ab]
        xp = [_dot(ab[i], bdiag(ab[i])) for i in cs]
        qs = [eye + a_ab[i] for i in cs]
        n = 2
        while n < CHUNK:
            yield
            xb = [bdiag(_mx(t)) for t in xp]
            if 2 * n < CHUNK:
                res = [_dot(_mx(jnp.concatenate([xp[i], qs[i]], axis=0)), xb[i]) for i in cs]
                xp = [res[i][:CHUNK] for i in cs]
                qs = [qs[i] + res[i][CHUNK:] for i in cs]
            else:
                upd = [_dot(_mx(qs[i]), xb[i]) for i in cs]
                qs = [qs[i] + upd[i] for i in cs]
            n *= 2
        out["inv"] = [_mx(t) for t in qs]

    def state_step(c, pre, state):
        ar, vf, rows = pre["ar"], pre["vf"], pre["rows"]
        asrs = [_dot(ar[i], bdiag_t(state[i])) for i in cs]
        yield
        rhs = [_mx(asrs[i][:CHUNK] + pre["kv"][i][:CHUNK]) for i in cs]
        u = [_dot(pre["inv"][i], bdiag(rhs[i])) for i in cs]
        yield
        yb = [_dot(pre["a_rb"][i], bdiag(_mx(u[i]))) for i in cs]
        uv_t = [_mx(jnp.concatenate([u[i], vf[i]], axis=0).T) for i in cs]
        full = [_dot(uv_t[i], pre["bk"][i]) * diag_f for i in cs]
        for i in cs:
            ds = full[i][0:HEAD]
            for h in range(1, SLAB // HEAD):
                ds = ds + full[i][h * HEAD:(h + 1) * HEAD]
            wc = wc_s[rows[i].stop - 1:rows[i].stop, sl[i]]
            state[i] = (state[i] + ds) * wc
            y_s[rows[i], sl[i]] = asrs[i][CHUNK:] + yb[i] + pre["kv"][i][CHUNK:]

    n_chunks = T // CHUNK
    state = [state_s[i] for i in cs]
    pre = {}
    for _ in state_free(0, pre):
        pass
    for c in range(n_chunks):
        nxt = {}
        streams = [state_step(c, pre, state)]
        if c + 1 < n_chunks:
            streams.append(state_free(c + 1, nxt))
        while streams:
            streams = [g for g in streams if next(g, streams) is not streams]
        pre = nxt
    for i in cs:
        state_s[i] = state[i]

    for b in range(B):
        rows = slice(b * T, (b + 1) * T)
        y = y_s[rows, :]
        mean = gsum(y) * (1.0 / HEAD)
        d = y - mean
        var = gsum(d * d) * (1.0 / HEAD)
        yn = d * lax.rsqrt(var + LNX_EPS) * lw_ref[...] + lb_ref[...]
        y_ref[b] = ((yn + bonus_s[rows, :]) * g_s[rows, :]).astype(y_ref.dtype)


def _rwkv(rw, mu, w0, w2p, a0, a2p, g2p, k_k, k_a, r_k, lnx_w, lnx_b, gmat, ltri, B, S, T):
    const = lambda i: (0, 0)
    vec = pl.BlockSpec((1, R_WIDTH), const)
    big = lambda dt: pltpu.VMEM((B * T, R_WIDTH), dt)
    y = pl.pallas_call(
        _rwkv_kernel,
        grid=(S // T,),
        in_specs=[
            pl.BlockSpec((B, T, R_COLS_PAD), lambda i: (0, i, 0)),
            pl.BlockSpec((1, R_COLS_PAD), const),
            vec,
            pl.BlockSpec((LANES, R_WIDTH), const),
            vec,
            pl.BlockSpec((LANES, R_WIDTH), const),
            pl.BlockSpec((GLAT_PAD, R_WIDTH), const),
            vec, vec, vec, vec, vec,
            pl.BlockSpec((R_WIDTH, R_WIDTH), const),
            pl.BlockSpec((T, T), const),
        ],
        out_specs=pl.BlockSpec((B, T, R_WIDTH), lambda i: (0, i, 0)),
        out_shape=jax.ShapeDtypeStruct((B, S, R_WIDTH), MXU_DTYPE),
        scratch_shapes=[
            pltpu.VMEM((B * (R_WIDTH // SLAB), HEAD, SLAB), F32),
            pltpu.VMEM((B, 8, R_COLS_PAD), F32),
            big(MXU_DTYPE), big(MXU_DTYPE), big(MXU_DTYPE), big(MXU_DTYPE),
            big(F32), big(F32), big(F32), big(F32), big(F32),
        ],
        compiler_params=pltpu.CompilerParams(
            dimension_semantics=("arbitrary",), vmem_limit_bytes=VMEM_LIMIT),
        name="rwkv",
    )(rw.reshape(B, S, R_COLS_PAD), mu, w0, w2p, a0, a2p, g2p, k_k, k_a, r_k, lnx_w, lnx_b, gmat, ltri)
    return y.reshape(B * S, R_WIDTH)


def _cmp_kernel(x_ref, w1_ref, b1_ref, w2_ref, pe_ref, pos_ref, freq_ref, sgn_ref, o_ref, ot_ref):
    n_blk = x_ref.shape[0]
    half = x_ref.shape[1]
    j = pl.program_id(1)
    xb = _mx(x_ref[...])
    w1 = w1_ref[...]
    first_half = _dot(xb, w1[:half])
    second_half = pltpu.roll(_dot(xb, w1[half:]), n_blk - 1, 0)
    bias = _dot(_mx(pe_ref[...]), w1)[0:1] + b1_ref[...]
    hdn = first_half + second_half + bias
    hdn = hdn * jax.nn.sigmoid(hdn)
    out = _dot(_mx(hdn), w2_ref[...])
    cos, sin = _rope_tables(pos_ref[...], freq_ref[...], sgn_ref[...])
    lane = lax.broadcasted_iota(jnp.int32, (1, LANES), 1)
    roped = _rope128(out, cos, sin, (lane % HEAD) < ROT_HALF)
    res = jnp.where(j < N_KV_HEADS, roped, out)
    row = lax.broadcasted_iota(jnp.int32, (n_blk, 1), 0)
    res = jnp.where(row == n_blk - 1, 0.0, res)
    o_ref[...] = res[:, :HEAD].astype(o_ref.dtype)
    ot_ref[...] = res.T[:HEAD].astype(ot_ref.dtype)


def _compress(xblk, w1s, b1s, w2s, pes, pos_c, freq, sgn, B, n_blk):
    half = xblk.shape[-1]
    return pl.pallas_call(
        _cmp_kernel,
        grid=(B, 2 * N_KV_HEADS),
        in_specs=[
            pl.BlockSpec((None, None, n_blk, half), lambda b, j: (b, j, 0, 0)),
            pl.BlockSpec((None, 2 * half, CMP_HIDDEN), lambda b, j: (j // N_KV_HEADS, 0, 0)),
            pl.BlockSpec((None, 1, CMP_HIDDEN), lambda b, j: (j // N_KV_HEADS, 0, 0)),
            pl.BlockSpec((None, CMP_HIDDEN, LANES), lambda b, j: (j // N_KV_HEADS, 0, 0)),
            pl.BlockSpec((None, 8, 2 * half), lambda b, j: (j // N_KV_HEADS, 0, 0)),
            pl.BlockSpec((None, n_blk, 1), lambda b, j: (b, 0, 0)),
            pl.BlockSpec((1, LANES), lambda b, j: (0, 0)),
            pl.BlockSpec((1, LANES), lambda b, j: (0, 0)),
        ],
        out_specs=[
            pl.BlockSpec((None, None, n_blk, HEAD), lambda b, j: (b, j, 0, 0)),
            pl.BlockSpec((None, None, HEAD, n_blk), lambda b, j: (b, j, 0, 0)),
        ],
        out_shape=[
            jax.ShapeDtypeStruct((B, 2 * N_KV_HEADS, n_blk, HEAD), MXU_DTYPE),
            jax.ShapeDtypeStruct((B, 2 * N_KV_HEADS, HEAD, n_blk), MXU_DTYPE),
        ],
        compiler_params=pltpu.CompilerParams(
            dimension_semantics=("parallel", "parallel"), vmem_limit_bytes=VMEM_LIMIT),
        name="compress",
    )(xblk, w1s, b1s, w2s, pes, pos_c, freq, sgn)


def _nsa_kernel(q_ref, kc_ref, vct_ref, ks_ref, vst_ref, kw_ref, vwt_ref, g_ref, o_ref,
                sel_s, ocmp_s, owin_s, sbuf_s, *, n_top, levels):
    n_blk = kc_ref.shape[0]
    n_sel = sel_s.shape[0]
    hk = pl.program_id(1)
    qb = pl.program_id(2)
    start = qb * Q_BLOCK
    W = GQA * Q_BLOCK

    q_t = q_ref[...].astype(F32).T
    qt = _mx(jnp.concatenate([q_t[g * HEAD:(g + 1) * HEAD] for g in range(GQA)], axis=1))
    tile4 = lambda t: jnp.concatenate([t] * GQA, axis=1)
    qi = lax.broadcasted_iota(jnp.int32, (1, Q_BLOCK), 1)
    t_q = start + qi
    t_w = tile4(t_q)

    tiles = GROUP // KV_TILE

    def compressed_and_select(n_c, n_s):
        w0 = pl.multiple_of(jnp.maximum(start - WINDOW, 0), Q_BLOCK)
        n_win = WINDOW + Q_BLOCK
        s_win = _dot(kw_ref[pl.ds(w0, n_win), :], qt)

        c_end = lax.broadcasted_iota(jnp.int32, (n_c, 1), 0) * CMP_STRIDE + (CMP_BLOCK - 1)
        sc = _dot(kc_ref[0:n_c, :], qt) + tile4(jnp.where(c_end <= t_q, 0.0, MASK))
        q_plain = jnp.concatenate([qt, jnp.zeros((LANES - HEAD, W), MXU_DTYPE)], axis=0)
        s_first = [_dot(ks_ref[i * KV_TILE:(i + 1) * KV_TILE, :], q_plain) for i in range(tiles)]

        tok = w0 + lax.broadcasted_iota(jnp.int32, (n_win, 1), 0)
        in_win = (tok <= t_q) & (tok > t_q - WINDOW)
        s_win = s_win + tile4(jnp.where(in_win, 0.0, MASK))
        p_win = jnp.exp2(s_win - jnp.max(s_win, axis=0, keepdims=True))
        l_win = jnp.sum(p_win, axis=0, keepdims=True)
        p_win = _mx(p_win)
        acc_win = jnp.zeros((HEAD, W), F32)
        for i in range(n_win // Q_BLOCK):
            acc_win = acc_win + _dot(vwt_ref[w0 // Q_BLOCK + i], p_win[i * Q_BLOCK:(i + 1) * Q_BLOCK])
        owin_s[...] = acc_win * (1.0 / l_win)

        m = jnp.max(sc, axis=0, keepdims=True)
        p = jnp.exp2(sc - m)
        l = jnp.sum(p, axis=0, keepdims=True)
        pn = p * jnp.where(t_w >= CMP_BLOCK - 1, 1.0 / l, 0.0)
        ocmp_s[...] = _dot(vct_ref[:, 0:n_c], _mx(pn))

        ps = pn[:, 0:Q_BLOCK]
        for g in range(1, GQA):
            ps = ps + pn[:, g * Q_BLOCK:(g + 1) * Q_BLOCK]
        hi = _mx(ps)
        lo = _mx(ps - hi.astype(F32))
        j_col = lax.broadcasted_iota(jnp.int32, (n_s, 1), 0)
        c_row = lax.broadcasted_iota(jnp.int32, (1, n_c), 1)
        cover = ((c_row * CMP_STRIDE < j_col * SEL_BLOCK + SEL_BLOCK)
                 & (c_row * CMP_STRIDE + CMP_BLOCK > j_col * SEL_BLOCK))
        cover = jnp.where(cover, 1.0, 0.0).astype(MXU_DTYPE)
        imp = _dot(cover, hi) + _dot(cover, lo)

        cur = t_q // SEL_BLOCK
        forced = (j_col == 0) | (j_col == cur) | (j_col == cur - 1)
        score = jnp.where(forced, REMOVED, jnp.where(j_col <= cur, imp, MASK))
        for _ in range(max(n_top - 3, 0)):
            mx = jnp.max(score, axis=0, keepdims=True)
            idx = jnp.min(jnp.where(score == mx, j_col, n_s), axis=0, keepdims=True)
            score = jnp.where(j_col == idx, REMOVED, score)
        sel = jnp.where(score == REMOVED, 1.0, 0.0)
        sel_s[0:n_s, :] = sel

        neg = jnp.where(sel[0:SEL_GROUP] > 0.5, 0.0, MASK)
        per_tile = KV_TILE // SEL_BLOCK
        for i in range(tiles):
            bias = jnp.concatenate([jnp.broadcast_to(neg[r:r + 1], (SEL_BLOCK, Q_BLOCK))
                                    for r in range(i * per_tile, (i + 1) * per_tile)], axis=0)
            sbuf_s[0, i * KV_TILE:(i + 1) * KV_TILE, :] = s_first[i] + tile4(bias)

    lo_qb = 0
    for hi_qb in levels:
        n_c = min(hi_qb * (Q_BLOCK // CMP_STRIDE), n_blk)
        n_s = min(hi_qb * (Q_BLOCK // SEL_BLOCK), n_sel)
        pl.when((qb >= lo_qb) & (qb < hi_qb))(functools.partial(compressed_and_select, n_c, n_s))
        lo_qb = hi_qb
    o_cmp = ocmp_s[...]

    zero_rows = jnp.zeros((LANES - HEAD - 2 * SEL_GROUP, W), MXU_DTYPE)

    def group_scores(j):
        grp = sel_s[pl.ds(pl.multiple_of(j * SEL_GROUP, SEL_GROUP), SEL_GROUP), :]
        neg = jnp.where(grp > 0.5, 0.0, MASK)
        neg = tile4(jnp.concatenate([neg, jnp.zeros_like(neg)], axis=0))
        q_aug = jnp.concatenate([qt, _mx(neg), zero_rows], axis=0)
        r0 = pl.multiple_of(j * GROUP, GROUP)
        return [_dot(ks_ref[pl.ds(r0 + i * KV_TILE, KV_TILE), :], q_aug) for i in range(tiles)]

    def put_scores(slot, j):
        for i, s in enumerate(group_scores(j)):
            sbuf_s[slot, i * KV_TILE:(i + 1) * KV_TILE, :] = s

    def online(s, vt_tile, carry):
        m_, acc = carry
        m_new = jnp.maximum(m_, jnp.max(s, axis=0, keepdims=True))
        alpha = jnp.exp2(m_ - m_new)
        return m_new, alpha * acc + _dot(vt_tile, _mx(jnp.exp2(s - m_new)))

    def consume(slot, j, carry):
        for i in range(tiles):
            s = sbuf_s[slot, i * KV_TILE:(i + 1) * KV_TILE, :]
            carry = online(s, vst_ref[j * tiles + i], carry)
        return carry

    def pair_body(jp, carry):
        j = 2 * jp
        put_scores(1, j + 1)
        carry = consume(0, j, carry)
        put_scores(0, j + 2)
        return consume(1, j + 1, carry)

    def single_body(j, carry):
        carry = consume(0, j, carry)
        put_scores(0, j + 1)
        return carry

    init = (jnp.full((1, W), MASK, F32), jnp.zeros((HEAD + ONES_ROWS, W), F32))
    last = qb // (GROUP // Q_BLOCK)
    quad_body = lambda jq, c: pair_body(2 * jq + 1, pair_body(2 * jq, c))
    oct_body = lambda jo, c: quad_body(2 * jo + 1, quad_body(2 * jo, c))
    carry = lax.fori_loop(0, last // 8, oct_body, init)
    carry = lax.fori_loop(2 * (last // 8), last // 4, quad_body, carry)
    carry = lax.fori_loop(2 * (last // 4), last // 2, pair_body, carry)
    carry = lax.fori_loop(2 * (last // 2), last, single_body, carry)

    k_row = lax.broadcasted_iota(jnp.int32, (KV_TILE, 1), 0)
    for i in range(tiles):
        tok = last * GROUP + i * KV_TILE + k_row
        causal = tile4(jnp.where(tok <= t_q, 0.0, MASK))
        s = sbuf_s[0, i * KV_TILE:(i + 1) * KV_TILE, :] + causal
        carry = online(s, vst_ref[last * tiles + i], carry)
    _, acc_sel = carry
    o_sel = acc_sel[:HEAD] * (1.0 / acc_sel[HEAD:HEAD + 1])
    o_win = owin_s[...]

    outs = []
    gates = g_ref[...]
    per_kv = GQA * 3
    gate = lambda r: jnp.where(hk == 0, gates[r:r + 1], gates[per_kv + r:per_kv + r + 1])
    for g in range(GQA):
        cols = slice(g * Q_BLOCK, (g + 1) * Q_BLOCK)
        o_g = (gate(g * 3) * o_cmp[:, cols] + gate(g * 3 + 1) * o_sel[:, cols]
               + gate(g * 3 + 2) * o_win[:, cols])
        outs.append(o_g)
    o_ref[...] = jnp.concatenate(outs, axis=0).T.astype(o_ref.dtype)


def _nsa(q, kc, vct, ks, vst, kw, vwt, gt, B, S):
    BS = q.shape[0]
    nqb = S // Q_BLOCK
    n_blk = S // CMP_STRIDE
    n_sel = S // SEL_BLOCK
    n_top = min(SEL_TOPK, n_sel)
    levels = tuple(b for b in (nqb // 4, nqb // 2, 3 * nqb // 4) if b > 0) + (nqb,)
    per_head = lambda *blk: pl.BlockSpec((None, None) + blk, lambda b, h, i: (b, h) + (0,) * len(blk))
    qspec = pl.BlockSpec((Q_BLOCK, GQA * HEAD), lambda b, h, i: (b * nqb + i, h))
    return pl.pallas_call(
        functools.partial(_nsa_kernel, n_top=n_top, levels=levels),
        grid=(B, N_KV_HEADS, nqb),
        in_specs=[
            qspec,
            per_head(n_blk, HEAD),
            per_head(HEAD, n_blk),
            per_head(S, LANES),
            per_head(S // KV_TILE, HEAD + ONES_ROWS, KV_TILE),
            per_head(S, HEAD),
            per_head(nqb, HEAD, LANES),
            pl.BlockSpec((None, GATE_ROWS, Q_BLOCK), lambda b, h, i: (b, 0, i)),
        ],
        out_specs=qspec,
        out_shape=jax.ShapeDtypeStruct((BS, N_WIDTH), MXU_DTYPE),
        scratch_shapes=[
            pltpu.VMEM((n_sel, Q_BLOCK), F32),
            pltpu.VMEM((HEAD, GQA * Q_BLOCK), F32),
            pltpu.VMEM((HEAD, GQA * Q_BLOCK), F32),
            pltpu.VMEM((2, GROUP, GQA * Q_BLOCK), F32),
        ],
        compiler_params=pltpu.CompilerParams(
            dimension_semantics=("parallel", "parallel", "arbitrary"), vmem_limit_bytes=VMEM_LIMIT),
        name="nsa",
    )(q, kc, vct, ks, vst, kw, vwt, gt)


def _ffn_kernel(x_ref, yr_ref, yn_ref, wo_ref, nf_ref, wg_ref, wu_ref, wd_ref, nfin_ref, o_ref,
                h1_s, xn_s, acc_s):
    f = pl.program_id(1)

    @pl.when(f == 0)
    def _():
        mix = _dot(yr_ref[...], wo_ref[:R_WIDTH]) + _dot(yn_ref[...], wo_ref[R_WIDTH:])
        h1 = x_ref[...] + mix
        h1_s[...] = h1
        xn_s[...] = _rms(h1, nf_ref[...]).astype(xn_s.dtype)
        acc_s[...] = jnp.zeros_like(acc_s)

    xn = xn_s[...]
    gate = _dot(xn, wg_ref[...])
    up = _dot(xn, wu_ref[...])
    acc_s[...] += _dot(_mx(gate * jax.nn.sigmoid(gate) * up), wd_ref[...])

    @pl.when(f == pl.num_programs(1) - 1)
    def _():
        o_ref[...] = _rms(h1_s[...] + acc_s[...], nfin_ref[...])


def _ffn(x2, y_r, y_n, w_out, nf, w_gate, w_up, w_down, nfin, tm, tf):
    BS, D = x2.shape
    d_ff = w_gate.shape[1]
    row = lambda i, f: (i, 0)
    const = lambda i, f: (0, 0)
    return pl.pallas_call(
        _ffn_kernel,
        grid=(BS // tm, d_ff // tf),
        in_specs=[
            pl.BlockSpec((tm, D), row),
            pl.BlockSpec((tm, R_WIDTH), row),
            pl.BlockSpec((tm, N_WIDTH), row),
            pl.BlockSpec((R_WIDTH + N_WIDTH, D), const),
            pl.BlockSpec((1, D), const),
            pl.BlockSpec((D, tf), lambda i, f: (0, f)),
            pl.BlockSpec((D, tf), lambda i, f: (0, f)),
            pl.BlockSpec((tf, D), lambda i, f: (f, 0)),
            pl.BlockSpec((1, D), const),
        ],
        out_specs=pl.BlockSpec((tm, D), row),
        out_shape=jax.ShapeDtypeStruct((BS, D), F32),
        scratch_shapes=[pltpu.VMEM((tm, D), F32), pltpu.VMEM((tm, D), MXU_DTYPE), pltpu.VMEM((tm, D), F32)],
        compiler_params=pltpu.CompilerParams(
            dimension_semantics=("parallel", "arbitrary"), vmem_limit_bytes=VMEM_LIMIT),
        name="ffn",
    )(x2, y_r, y_n, w_out, nf, w_gate, w_up, w_down, nfin)


def _pad_to(a, n, axis):
    pad = [(0, 0)] * a.ndim
    pad[axis] = (0, n - a.shape[axis])
    return jnp.pad(a, pad)


def _rope_rows():
    lane = jnp.arange(LANES) % HEAD
    inv_freq = jnp.power(jnp.float32(ROPE_THETA), -jnp.arange(ROT_HALF, dtype=F32) * 2.0 / ROT_DIM)
    freq = jnp.where(lane < ROT_DIM, inv_freq[lane % ROT_HALF], 0.0).astype(F32)[None, :]
    sgn = jnp.where(lane < ROT_HALF, -1.0, jnp.where(lane < ROT_DIM, 1.0, 0.0)).astype(F32)[None, :]
    return freq, sgn


def _layer(h, positions, norm_mix, w_in, mu_shift, w0, w2, a0, a2, g2, k_k, k_a, r_k, lnx_w, lnx_b,
           pe_k, wk1, bk1, wk2, pe_v, wv1, bv1, wv2, w_out, norm_ffn, w_gate, w_up, w_down, norm_final):
    B, S, D = h.shape
    BS = B * S
    x2 = h.reshape(BS, D)
    freq, sgn = _rope_rows()
    row = lambda a: a.reshape(1, -1).astype(F32)

    w_r = _mx(_pad_to(w_in[:, :R_COLS], R_COLS_PAD, 1))
    w_n = _mx(_pad_to(w_in[:, R_COLS:], N_COLS_PAD, 1))
    tm = min(512, S)
    rw, q, kvc, ks, vst, kw, vwt, gt = _proj(
        x2, positions.reshape(BS, 1), row(norm_mix), freq, sgn, w_r, w_n, B, S, tm)

    T = min(256, S)
    head_of = jnp.arange(R_WIDTH) // HEAD
    gmat = (head_of[:, None] == head_of[None, :]).astype(MXU_DTYPE)
    tt = jnp.arange(T)
    ltri = ((tt[:, None] // CHUNK == tt[None, :] // CHUNK) & (tt[:, None] >= tt[None, :])).astype(MXU_DTYPE)
    w2p = _mx(_pad_to(w2, LANES, 0))
    a2p = _mx(jnp.concatenate([jnp.zeros((DECAY_LORA, R_WIDTH), F32), a2], axis=0))
    g2p = _mx(_pad_to(g2, GLAT_PAD, 0))
    y_r = _rwkv(rw, row(_pad_to(mu_shift, R_COLS_PAD, 0)), row(w0), w2p, row(a0), a2p, g2p,
                row(k_k), row(k_a), row(r_k), row(lnx_w), row(lnx_b), gmat, ltri, B, S, T)

    n_blk = S // CMP_STRIDE
    xblk = kvc.reshape(B, n_blk, CMP_STRIDE, 2 * N_KV_HEADS, HEAD).transpose(0, 3, 1, 2, 4)
    xblk = xblk.reshape(B, 2 * N_KV_HEADS, n_blk, CMP_STRIDE * HEAD)
    w1s = _mx(jnp.stack([wk1, wv1]))
    b1s = jnp.stack([bk1, bv1]).reshape(2, 1, CMP_HIDDEN).astype(F32)
    w2s = _mx(_pad_to(jnp.stack([wk2, wv2]), LANES, 2))
    pes = _pad_to(jnp.stack([pe_k, pe_v]).reshape(2, 1, CMP_BLOCK * HEAD), 8, 1).astype(F32)
    pos_c = _pad_to(positions[:, CMP_BLOCK - 1::CMP_STRIDE], n_blk, 1).reshape(B, n_blk, 1)
    cmp_rows, cmp_t = _compress(xblk, w1s, b1s, w2s, pes, pos_c, freq, sgn, B, n_blk)
    kc = cmp_rows[:, :N_KV_HEADS]
    vct = cmp_t[:, N_KV_HEADS:]

    y_n = _nsa(q, kc, vct, ks, vst, kw, vwt, gt, B, S)

    d_ff = w_gate.shape[1]
    tf = d_ff // 2 if (d_ff // 2) % LANES == 0 else d_ff
    out = _ffn(x2, y_r, y_n, _mx(w_out), row(norm_ffn), _mx(w_gate), _mx(w_up), _mx(w_down),
               row(norm_final), min(512, BS), tf)
    return out.reshape(B, S, D)


def kernel(x, positions, norm_mix, w_in, mu_shift, w0, w2, a0, a2, g2, k_k, k_a, r_k, lnx_w, lnx_b, pe_k, wk1, bk1, wk2, pe_v, wv1, bv1, wv2, w_out, norm_ffn, w_gate, w_up, w_down, norm_final):
    assert norm_mix.shape[0] == 1, "single-layer block"
    l = 0
    return _layer(x, positions, norm_mix[l], w_in[l], mu_shift[l], w0[l], w2[l], a0[l], a2[l], g2[l],
                  k_k[l], k_a[l], r_k[l].reshape(-1), lnx_w[l], lnx_b[l], pe_k[l], wk1[l], bk1[l], wk2[l],
                  pe_v[l], wv1[l], bv1[l], wv2[l], w_out[l], norm_ffn[l], w_gate[l], w_up[l], w_down[l],
                  norm_final)
```

```python
import functools

import jax
import jax.numpy as jnp
from jax import lax
from jax.experimental import pallas as pl
from jax.experimental.pallas import tpu as pltpu

F32 = jnp.float32
MXU_DTYPE = jnp.bfloat16

R_HEADS = 8
HEAD = 64
R_WIDTH = R_HEADS * HEAD
DECAY_LORA = 64
ICLR_LORA = 64
GATE_LORA = 160
LNX_EPS = 64e-5
R_COLS = 3 * R_WIDTH + DECAY_LORA + ICLR_LORA + GATE_LORA
N_Q_HEADS = 8
N_KV_HEADS = 2
GQA = N_Q_HEADS // N_KV_HEADS
N_WIDTH = N_Q_HEADS * HEAD
KV_WIDTH = N_KV_HEADS * HEAD
CMP_BLOCK = 32
CMP_STRIDE = 16
CMP_HIDDEN = 256
SEL_BLOCK = 64
SEL_TOPK = 16
WINDOW = 512
Q_BLOCK = 128
ROPE_THETA = 500000.0
ROT_DIM = HEAD // 4
ROT_HALF = ROT_DIM // 2
NORM_EPS = 1e-6
MASK = -1e30
FORCE = 1e6
REMOVED = -(2.0 ** 127)

LANES = 128
R_COLS_PAD = 1920
GLAT_PAD = 256
N_COLS_PAD = N_WIDTH + 6 * KV_WIDTH + LANES
GATE_ROWS = 32
CHUNK = 64
SLAB = 4 * HEAD
SEL_GROUP = 8
GROUP = SEL_GROUP * SEL_BLOCK
KV_TILE = 256
ONES_ROWS = 16
LOG2E = 1.4426950408889634
EXP_M05 = 0.6065306597126334
CMP_VARIANTS = 8
PROJ_SUB = 256
FFN_SUB = 256
VMEM_LIMIT = 56 * 1024 * 1024


def _dot(a, b):
    return jnp.dot(a, b, preferred_element_type=F32)


def _dot_nt(a, b):
    return lax.dot_general(a, b, (((1,), (1,)), ((), ())), preferred_element_type=F32)


def _mx(a):
    return a.astype(MXU_DTYPE)


def _rms(x, w):
    ms = jnp.mean(x * x, axis=-1, keepdims=True)
    return x * lax.rsqrt(ms + NORM_EPS) * w


def _rope_tables(pos_col, freq_row, sgn_row):
    ang = pos_col.astype(F32) * freq_row
    return jnp.cos(ang), jnp.sin(ang) * sgn_row


def _rope128(t, cos, sin, first):
    sh = jnp.where(first, pltpu.roll(t, LANES - ROT_HALF, 1), pltpu.roll(t, ROT_HALF, 1))
    return t * cos + sh * sin


def _proj_kernel(x_ref, pos_ref, nw_ref, freq_ref, sgn_ref, wr_ref, wn_ref,
                 rw_ref, q_ref, kvc_ref, ks_ref, vst_ref, kw_ref, vwt_ref, gt_ref):
    tm = x_ref.shape[0]
    sub = min(tm, PROJ_SUB)
    lane = lax.broadcasted_iota(jnp.int32, (1, LANES), 1)
    first = (lane % HEAD) < ROT_HALF
    ones_rows = jnp.where(lax.broadcasted_iota(jnp.int32, (ONES_ROWS, KV_TILE), 0) == 0, 1.0, 0.0)
    scale = HEAD ** -0.5 * LOG2E

    for r0 in range(0, tm, sub):
        rows = slice(r0, r0 + sub)
        xn = _mx(_rms(x_ref[rows, :], nw_ref[...]))
        rw_ref[rows, :] = _dot(xn, wr_ref[...])
        pn = _dot(xn, wn_ref[...])
        cos, sin = _rope_tables(pos_ref[rows, :], freq_ref[...], sgn_ref[...])
        rope = lambda t: _rope128(t, cos, sin, first)
        q = [rope(pn[:, i * LANES:(i + 1) * LANES]) * scale for i in range(N_WIDTH // LANES)]
        q_ref[rows, :] = jnp.concatenate(q, axis=1).astype(q_ref.dtype)
        o = N_WIDTH
        kvc_ref[rows, :] = pn[:, o:o + 2 * KV_WIDTH]
        o += 2 * KV_WIDTH

        row_g = pl.program_id(1) * tm + r0 + lax.broadcasted_iota(jnp.int32, (sub, 1), 0)
        onehot = jnp.where(lane == HEAD + (row_g // SEL_BLOCK) % SEL_GROUP, 1.0, 0.0)
        kk = rope(pn[:, o:o + KV_WIDTH])
        vt = rope(pn[:, o + KV_WIDTH:o + 2 * KV_WIDTH]).T
        o += 2 * KV_WIDTH
        for h in range(N_KV_HEADS):
            k_h = kk if h == 0 else pltpu.roll(kk, HEAD, 1)
            ks_ref[h, rows, :] = jnp.where(lane < HEAD, k_h, onehot).astype(ks_ref.dtype)
            for i in range(sub // KV_TILE):
                v_t = vt[h * HEAD:(h + 1) * HEAD, i * KV_TILE:(i + 1) * KV_TILE]
                vst_ref[h, r0 // KV_TILE + i] = jnp.concatenate([v_t, ones_rows], axis=0).astype(vst_ref.dtype)

        kk = rope(pn[:, o:o + KV_WIDTH])
        vt = rope(pn[:, o + KV_WIDTH:o + 2 * KV_WIDTH]).T
        o += 2 * KV_WIDTH
        for h in range(N_KV_HEADS):
            kw_ref[h, rows, :] = kk[:, h * HEAD:(h + 1) * HEAD].astype(kw_ref.dtype)
            for i in range(sub // LANES):
                vwt_ref[h, r0 // LANES + i] = vt[h * HEAD:(h + 1) * HEAD,
                                                 i * LANES:(i + 1) * LANES].astype(vwt_ref.dtype)
        gt_ref[:, rows] = jax.nn.sigmoid(pn[:, o:o + LANES]).T[:GATE_ROWS]


def _proj(x2, pos_col, nw, freq, sgn, w_r, w_n, B, S, tm):
    BS, D = x2.shape
    nt = S // tm
    nkt = S // LANES
    row = lambda b, i: (b * nt + i, 0)
    const = lambda b, i: (0, 0)
    head_rows = lambda width: pl.BlockSpec((None, N_KV_HEADS, tm, width), lambda b, i: (b, 0, i, 0))
    head_tiles = lambda rows, width: pl.BlockSpec(
        (None, N_KV_HEADS, tm // width, rows, width), lambda b, i: (b, 0, i, 0, 0))
    kv = lambda *shape: jax.ShapeDtypeStruct((B, N_KV_HEADS) + shape, MXU_DTYPE)
    return pl.pallas_call(
        _proj_kernel,
        grid=(B, nt),
        in_specs=[
            pl.BlockSpec((tm, D), row),
            pl.BlockSpec((tm, 1), row),
            pl.BlockSpec((1, D), const),
            pl.BlockSpec((1, LANES), const),
            pl.BlockSpec((1, LANES), const),
            pl.BlockSpec((D, R_COLS_PAD), const),
            pl.BlockSpec((D, N_COLS_PAD), const),
        ],
        out_specs=[
            pl.BlockSpec((tm, R_COLS_PAD), row),
            pl.BlockSpec((tm, N_WIDTH), row),
            pl.BlockSpec((tm, 2 * KV_WIDTH), row),
            head_rows(LANES), head_tiles(HEAD + ONES_ROWS, KV_TILE),
            head_rows(HEAD), head_tiles(HEAD, LANES),
            pl.BlockSpec((None, GATE_ROWS, tm), lambda b, i: (b, 0, i)),
        ],
        out_shape=[
            jax.ShapeDtypeStruct((BS, R_COLS_PAD), F32),
            jax.ShapeDtypeStruct((BS, N_WIDTH), MXU_DTYPE),
            jax.ShapeDtypeStruct((BS, 2 * KV_WIDTH), F32),
            kv(S, LANES), kv(S // KV_TILE, HEAD + ONES_ROWS, KV_TILE),
            kv(S, HEAD), kv(nkt, HEAD, LANES),
            jax.ShapeDtypeStruct((B, GATE_ROWS, S), F32),
        ],
        compiler_params=pltpu.CompilerParams(
            dimension_semantics=("parallel", "parallel"), vmem_limit_bytes=VMEM_LIMIT),
        name="proj",
    )(x2, pos_col, nw, freq, sgn, w_r, w_n)


def _rwkv_kernel(rw_ref, mu_ref, w0_ref, w2_ref, a0_ref, a2_ref, g2_ref, kk_ref, ka_ref, rk_ref,
                 lw_ref, lb_ref, gmat_ref, ltri_ref, y_ref,
                 state_s, prev_s, at_s, rt_s, bt_s, kt_s, v_s, wc_s, y_s, bonus_s, g_s):
    B, T = rw_ref.shape[0], rw_ref.shape[1]

    @pl.when(pl.program_id(0) == 0)
    def _():
        state_s[...] = jnp.zeros_like(state_s)
        prev_s[...] = jnp.zeros_like(prev_s)

    gmat = gmat_ref[...]

    def gsum(t):
        return _dot(_mx(t), gmat)

    def prepare(b):
        rows = slice(b * T, (b + 1) * T)
        x = rw_ref[b]
        row = lax.broadcasted_iota(jnp.int32, (T, 1), 0)
        xs = jnp.where(row == 0, prev_s[b, 0:1, :], pltpu.roll(x, 1, 0))
        prev_s[b, 0:1, :] = x[T - 1:T, :]
        xm = x + (xs - x) * mu_ref[...]
        r = xm[:, 0:R_WIDTH]
        k = xm[:, R_WIDTH:2 * R_WIDTH]
        v = xm[:, 2 * R_WIDTH:3 * R_WIDTH]
        lat = xm[:, 3 * R_WIDTH:3 * R_WIDTH + LANES]
        glat = xm[:, 3 * R_WIDTH + LANES:3 * R_WIDTH + LANES + GLAT_PAD]

        wv = w0_ref[...] + _dot(_mx(jnp.tanh(lat)), w2_ref[...])
        ld = -EXP_M05 * jax.nn.sigmoid(wv)
        a = jax.nn.sigmoid(a0_ref[...] + _dot(_mx(lat), a2_ref[...]))
        g_s[rows, :] = _dot(_mx(jax.nn.sigmoid(glat)), g2_ref[...])
        kkv = k * kk_ref[...]
        kkn = kkv / jnp.maximum(jnp.sqrt(gsum(kkv * kkv)), 1e-12)
        k_mod = k * (1.0 + (a - 1.0) * ka_ref[...])
        bonus_s[rows, :] = gsum(r * k_mod * rk_ref[...]) * v

        ltri = ltri_ref[...]
        h1 = _mx(ld)
        h2 = _mx(ld - h1.astype(F32))
        cum = _dot(ltri, h1) + _dot(ltri, h2)
        w_inc = jnp.exp(cum)
        w_inv = jnp.exp(-cum)
        w_exc = jnp.exp(cum - ld)
        at_s[rows, :] = (-kkn * w_exc).astype(at_s.dtype)
        rt_s[rows, :] = (r * w_inc).astype(rt_s.dtype)
        bt_s[rows, :] = (kkn * a * w_inv).astype(bt_s.dtype)
        kt_s[rows, :] = (k_mod * w_inv).astype(kt_s.dtype)
        v_s[rows, :] = v
        wc_s[rows, :] = w_inc

    for b in range(B):
        prepare(b)

    rr = lax.broadcasted_iota(jnp.int32, (SLAB, SLAB), 0) // HEAD
    cc = lax.broadcasted_iota(jnp.int32, (SLAB, SLAB), 1) // HEAD
    diag_f = jnp.where(rr == cc, 1.0, 0.0)
    diag = _mx(diag_f)

    def bdiag(t):
        return jnp.concatenate([t] * (SLAB // HEAD), axis=0) * diag

    def bdiag_t(t):
        tt = jnp.concatenate([t, t], axis=0).T
        return _mx(jnp.concatenate([tt, tt], axis=1)) * diag

    ii = lax.broadcasted_iota(jnp.int32, (CHUNK, SLAB), 0)
    jj = lax.broadcasted_iota(jnp.int32, (CHUNK, SLAB), 1) % HEAD
    strict = ii > jj
    incl = ii >= jj
    eye = (ii == jj).astype(F32)
    chains = [(b, s) for b in range(B) for s in range(R_WIDTH // SLAB)]

    cs = range(len(chains))
    sl = [slice(s * SLAB, (s + 1) * SLAB) for _, s in chains]

    def state_free(c, out):
        rows = [slice(b * T + c * CHUNK, b * T + (c + 1) * CHUNK) for b, _ in chains]
        a_ = [at_s[rows[i], sl[i]] for i in cs]
        r_ = [rt_s[rows[i], sl[i]] for i in cs]
        b_ = [bt_s[rows[i], sl[i]] for i in cs]
        k_ = [kt_s[rows[i], sl[i]] for i in cs]
        vf = [v_s[rows[i], sl[i]] for i in cs]
        ar = [jnp.concatenate([a_[i], r_[i]], axis=0) for i in cs]
        out.update(rows=rows, vf=vf, ar=ar, bk=[jnp.concatenate([b_[i], k_[i]], axis=0) for i in cs])
        mb = [_dot(ar[i], bdiag_t(b_[i].astype(F32))) for i in cs]
        mk = [_dot(ar[i], bdiag_t(k_[i].astype(F32))) for i in cs]
        yield
        a_ab = [jnp.where(strict, mb[i][:CHUNK], 0.0) for i in cs]
        out["a_rb"] = [_mx(jnp.where(incl, mb[i][CHUNK:], 0.0)) for i in cs]
        a_k = [_mx(jnp.concatenate([jnp.where(strict, mk[i][:CHUNK], 0.0),
                                    jnp.where(incl, mk[i][CHUNK:], 0.0)], axis=0)) for i in cs]
        out["kv"] = [_dot(a_k[i], bdiag(_mx(vf[i]))) for i in cs]
        ab = [_mx(t) for t in a_ab]
        xp = [_dot(ab[i], bdiag(ab[i])) for i in cs]
        qs = [eye + a_ab[i] for i in cs]
        n = 2
        while n < CHUNK:
            yield
            xb = [bdiag(_mx(t)) for t in xp]
            if 2 * n < CHUNK:
                res = [_dot(_mx(jnp.concatenate([xp[i], qs[i]], axis=0)), xb[i]) for i in cs]
                xp = [res[i][:CHUNK] for i in cs]
                qs = [qs[i] + res[i][CHUNK:] for i in cs]
            else:
                upd = [_dot(_mx(qs[i]), xb[i]) for i in cs]
                qs = [qs[i] + upd[i] for i in cs]
            n *= 2
        out["inv"] = [_mx(t) for t in qs]

    def state_step(c, pre, state):
        ar, vf, rows = pre["ar"], pre["vf"], pre["rows"]
        asrs = [_dot(ar[i], bdiag_t(state[i])) for i in cs]
        yield
        rhs = [_mx(asrs[i][:CHUNK] + pre["kv"][i][:CHUNK]) for i in cs]
        u = [_dot(pre["inv"][i], bdiag(rhs[i])) for i in cs]
        yield
        yb = [_dot(pre["a_rb"][i], bdiag(_mx(u[i]))) for i in cs]
        uv_t = [_mx(jnp.concatenate([u[i], vf[i]], axis=0).T) for i in cs]
        full = [_dot(uv_t[i], pre["bk"][i]) * diag_f for i in cs]
        for i in cs:
            ds = full[i][0:HEAD]
            for h in range(1, SLAB // HEAD):
                ds = ds + full[i][h * HEAD:(h + 1) * HEAD]
            wc = wc_s[rows[i].stop - 1:rows[i].stop, sl[i]]
            state[i] = (state[i] + ds) * wc
            y_s[rows[i], sl[i]] = asrs[i][CHUNK:] + yb[i] + pre["kv"][i][CHUNK:]

    n_chunks = T // CHUNK
    state = [state_s[i] for i in cs]
    pre = {}
    for _ in state_free(0, pre):
        pass
    for c in range(n_chunks):
        nxt = {}
        streams = [state_step(c, pre, state)]
        if c + 1 < n_chunks:
            streams.append(state_free(c + 1, nxt))
        while streams:
            streams = [g for g in streams if next(g, streams) is not streams]
        pre = nxt
    for i in cs:
        state_s[i] = state[i]

    for b in range(B):
        rows = slice(b * T, (b + 1) * T)
        y = y_s[rows, :]
        mean = gsum(y) * (1.0 / HEAD)
        d = y - mean
        var = gsum(d * d) * (1.0 / HEAD)
        yn = d * lax.rsqrt(var + LNX_EPS) * lw_ref[...] + lb_ref[...]
        y_ref[b] = ((yn + bonus_s[rows, :]) * g_s[rows, :]).astype(y_ref.dtype)


def _rwkv(rw, mu, w0, w2p, a0, a2p, g2p, k_k, k_a, r_k, lnx_w, lnx_b, gmat, ltri, B, S, T):
    const = lambda i: (0, 0)
    vec = pl.BlockSpec((1, R_WIDTH), const)
    big = lambda dt: pltpu.VMEM((B * T, R_WIDTH), dt)
    y = pl.pallas_call(
        _rwkv_kernel,
        grid=(S // T,),
        in_specs=[
            pl.BlockSpec((B, T, R_COLS_PAD), lambda i: (0, i, 0)),
            pl.BlockSpec((1, R_COLS_PAD), const),
            vec,
            pl.BlockSpec((LANES, R_WIDTH), const),
            vec,
            pl.BlockSpec((LANES, R_WIDTH), const),
            pl.BlockSpec((GLAT_PAD, R_WIDTH), const),
            vec, vec, vec, vec, vec,
            pl.BlockSpec((R_WIDTH, R_WIDTH), const),
            pl.BlockSpec((T, T), const),
        ],
        out_specs=pl.BlockSpec((B, T, R_WIDTH), lambda i: (0, i, 0)),
        out_shape=jax.ShapeDtypeStruct((B, S, R_WIDTH), MXU_DTYPE),
        scratch_shapes=[
            pltpu.VMEM((B * (R_WIDTH // SLAB), HEAD, SLAB), F32),
            pltpu.VMEM((B, 8, R_COLS_PAD), F32),
            big(MXU_DTYPE), big(MXU_DTYPE), big(MXU_DTYPE), big(MXU_DTYPE),
            big(F32), big(F32), big(F32), big(F32), big(F32),
        ],
        compiler_params=pltpu.CompilerParams(
            dimension_semantics=("arbitrary",), vmem_limit_bytes=VMEM_LIMIT),
        name="rwkv",
    )(rw.reshape(B, S, R_COLS_PAD), mu, w0, w2p, a0, a2p, g2p, k_k, k_a, r_k, lnx_w, lnx_b, gmat, ltri)
    return y.reshape(B * S, R_WIDTH)


def _cmp_kernel(x_ref, w1_ref, b1_ref, w2_ref, pe_ref, pos_ref, freq_ref, sgn_ref, o_ref, ot_ref):
    n_blk = x_ref.shape[0]
    half = x_ref.shape[1]
    j = pl.program_id(1)
    xb = _mx(x_ref[...])
    w1 = w1_ref[...]
    first_half = _dot(xb, w1[:half])
    second_half = pltpu.roll(_dot(xb, w1[half:]), n_blk - 1, 0)
    bias = _dot(_mx(pe_ref[...]), w1)[0:1] + b1_ref[...]
    hdn = first_half + second_half + bias
    hdn = hdn * jax.nn.sigmoid(hdn)
    out = _dot(_mx(hdn), w2_ref[...])
    cos, sin = _rope_tables(pos_ref[...], freq_ref[...], sgn_ref[...])
    lane = lax.broadcasted_iota(jnp.int32, (1, LANES), 1)
    roped = _rope128(out, cos, sin, (lane % HEAD) < ROT_HALF)
    res = jnp.where(j < N_KV_HEADS, roped, out)
    row = lax.broadcasted_iota(jnp.int32, (n_blk, 1), 0)
    res = jnp.where(row == n_blk - 1, 0.0, res)
    o_ref[...] = res[:, :HEAD].astype(o_ref.dtype)
    ot_ref[...] = res.T[:HEAD].astype(ot_ref.dtype)


def _compress(xblk, w1s, b1s, w2s, pes, pos_c, freq, sgn, B, n_blk):
    half = xblk.shape[-1]
    return pl.pallas_call(
        _cmp_kernel,
        grid=(B, 2 * N_KV_HEADS),
        in_specs=[
            pl.BlockSpec((None, None, n_blk, half), lambda b, j: (b, j, 0, 0)),
            pl.BlockSpec((None, 2 * half, CMP_HIDDEN), lambda b, j: (j // N_KV_HEADS, 0, 0)),
            pl.BlockSpec((None, 1, CMP_HIDDEN), lambda b, j: (j // N_KV_HEADS, 0, 0)),
            pl.BlockSpec((None, CMP_HIDDEN, LANES), lambda b, j: (j // N_KV_HEADS, 0, 0)),
            pl.BlockSpec((None, 8, 2 * half), lambda b, j: (j // N_KV_HEADS, 0, 0)),
            pl.BlockSpec((None, n_blk, 1), lambda b, j: (b, 0, 0)),
            pl.BlockSpec((1, LANES), lambda b, j: (0, 0)),
            pl.BlockSpec((1, LANES), lambda b, j: (0, 0)),
        ],
        out_specs=[
            pl.BlockSpec((None, None, n_blk, HEAD), lambda b, j: (b, j, 0, 0)),
            pl.BlockSpec((None, None, HEAD, n_blk), lambda b, j: (b, j, 0, 0)),
        ],
        out_shape=[
            jax.ShapeDtypeStruct((B, 2 * N_KV_HEADS, n_blk, HEAD), MXU_DTYPE),
            jax.ShapeDtypeStruct((B, 2 * N_KV_HEADS, HEAD, n_blk), MXU_DTYPE),
        ],
        compiler_params=pltpu.CompilerParams(
            dimension_semantics=("parallel", "parallel"), vmem_limit_bytes=VMEM_LIMIT),
        name="compress",
    )(xblk, w1s, b1s, w2s, pes, pos_c, freq, sgn)


def _nsa_kernel(q_ref, kc_ref, vct_ref, ks_ref, vst_ref, kw_ref, vwt_ref, g_ref, o_ref,
                sel_s, ocmp_s, owin_s, sbuf_s, *, n_top, levels):
    n_blk = kc_ref.shape[0]
    n_sel = sel_s.shape[0]
    hk = pl.program_id(1)
    qb = pl.program_id(2)
    start = qb * Q_BLOCK
    W = GQA * Q_BLOCK

    q_t = q_ref[...].astype(F32).T
    qt = _mx(jnp.concatenate([q_t[g * HEAD:(g + 1) * HEAD] for g in range(GQA)], axis=1))
    tile4 = lambda t: jnp.concatenate([t] * GQA, axis=1)
    qi = lax.broadcasted_iota(jnp.int32, (1, Q_BLOCK), 1)
    t_q = start + qi
    t_w = tile4(t_q)

    tiles = GROUP // KV_TILE

    def compressed_and_select(n_c, n_s):
        w0 = pl.multiple_of(jnp.maximum(start - WINDOW, 0), LANES)
        n_win = WINDOW + Q_BLOCK
        s_win = _dot(kw_ref[pl.ds(w0, n_win), :], qt)

        c_end = lax.broadcasted_iota(jnp.int32, (n_c, 1), 0) * CMP_STRIDE + (CMP_BLOCK - 1)
        sc = _dot(kc_ref[0:n_c, :], qt) + tile4(jnp.where(c_end <= t_q, 0.0, MASK))
        q_plain = jnp.concatenate([qt, jnp.zeros((LANES - HEAD, W), MXU_DTYPE)], axis=0)
        s_first = [_dot(ks_ref[i * KV_TILE:(i + 1) * KV_TILE, :], q_plain) for i in range(tiles)]

        tok = w0 + lax.broadcasted_iota(jnp.int32, (n_win, 1), 0)
        in_win = (tok <= t_q) & (tok > t_q - WINDOW)
        s_win = s_win + tile4(jnp.where(in_win, 0.0, MASK))
        p_win = jnp.exp2(s_win - jnp.max(s_win, axis=0, keepdims=True))
        l_win = jnp.sum(p_win, axis=0, keepdims=True)
        p_win = _mx(p_win)
        acc_win = jnp.zeros((HEAD, W), F32)
        for i in range(n_win // LANES):
            acc_win = acc_win + _dot(vwt_ref[w0 // LANES + i], p_win[i * LANES:(i + 1) * LANES])
        owin_s[...] = acc_win * (1.0 / l_win)

        m = jnp.max(sc, axis=0, keepdims=True)
        p = jnp.exp2(sc - m)
        l = jnp.sum(p, axis=0, keepdims=True)
        pn = p * jnp.where(t_w >= CMP_BLOCK - 1, 1.0 / l, 0.0)
        ocmp_s[...] = _dot(vct_ref[:, 0:n_c], _mx(pn))

        ps = pn[:, 0:Q_BLOCK]
        for g in range(1, GQA):
            ps = ps + pn[:, g * Q_BLOCK:(g + 1) * Q_BLOCK]
        hi = _mx(ps)
        lo = _mx(ps - hi.astype(F32))
        j_col = lax.broadcasted_iota(jnp.int32, (n_s, 1), 0)
        c_row = lax.broadcasted_iota(jnp.int32, (1, n_c), 1)
        cover = ((c_row * CMP_STRIDE < j_col * SEL_BLOCK + SEL_BLOCK)
                 & (c_row * CMP_STRIDE + CMP_BLOCK > j_col * SEL_BLOCK))
        cover = jnp.where(cover, 1.0, 0.0).astype(MXU_DTYPE)
        imp = _dot(cover, hi) + _dot(cover, lo)

        cur = t_q // SEL_BLOCK
        forced = (j_col == 0) | (j_col == cur) | (j_col == cur - 1)
        score = jnp.where(forced, REMOVED, jnp.where(j_col <= cur, imp, MASK))
        for _ in range(max(n_top - 3, 0)):
            mx = jnp.max(score, axis=0, keepdims=True)
            idx = jnp.min(jnp.where(score == mx, j_col, n_s), axis=0, keepdims=True)
            score = jnp.where(j_col == idx, REMOVED, score)
        sel = jnp.where(score == REMOVED, 1.0, 0.0)
        sel_s[0:n_s, :] = sel

        neg = jnp.where(sel[0:SEL_GROUP] > 0.5, 0.0, MASK)
        per_tile = KV_TILE // SEL_BLOCK
        for i in range(tiles):
            bias = jnp.concatenate([jnp.broadcast_to(neg[r:r + 1], (SEL_BLOCK, Q_BLOCK))
                                    for r in range(i * per_tile, (i + 1) * per_tile)], axis=0)
            sbuf_s[0, i * KV_TILE:(i + 1) * KV_TILE, :] = s_first[i] + tile4(bias)

    lo_qb = 0
    for hi_qb in levels:
        n_c = min(hi_qb * (Q_BLOCK // CMP_STRIDE), n_blk)
        n_s = min(hi_qb * (Q_BLOCK // SEL_BLOCK), n_sel)
        pl.when((qb >= lo_qb) & (qb < hi_qb))(functools.partial(compressed_and_select, n_c, n_s))
        lo_qb = hi_qb
    o_cmp = ocmp_s[...]

    zero_rows = jnp.zeros((LANES - HEAD - 2 * SEL_GROUP, W), MXU_DTYPE)

    def group_scores(j):
        grp = sel_s[pl.ds(pl.multiple_of(j * SEL_GROUP, SEL_GROUP), SEL_GROUP), :]
        neg = jnp.where(grp > 0.5, 0.0, MASK)
        neg = tile4(jnp.concatenate([neg, jnp.zeros_like(neg)], axis=0))
        q_aug = jnp.concatenate([qt, _mx(neg), zero_rows], axis=0)
        r0 = pl.multiple_of(j * GROUP, GROUP)
        return [_dot(ks_ref[pl.ds(r0 + i * KV_TILE, KV_TILE), :], q_aug) for i in range(tiles)]

    def put_scores(slot, j):
        for i, s in enumerate(group_scores(j)):
            sbuf_s[slot, i * KV_TILE:(i + 1) * KV_TILE, :] = s

    def online(s, vt_tile, carry):
        m_, acc = carry
        m_new = jnp.maximum(m_, jnp.max(s, axis=0, keepdims=True))
        alpha = jnp.exp2(m_ - m_new)
        return m_new, alpha * acc + _dot(vt_tile, _mx(jnp.exp2(s - m_new)))

    def consume(slot, j, carry):
        for i in range(tiles):
            s = sbuf_s[slot, i * KV_TILE:(i + 1) * KV_TILE, :]
            carry = online(s, vst_ref[j * tiles + i], carry)
        return carry

    def pair_body(jp, carry):
        j = 2 * jp
        put_scores(1, j + 1)
        carry = consume(0, j, carry)
        put_scores(0, j + 2)
        return consume(1, j + 1, carry)

    def single_body(j, carry):
        carry = consume(0, j, carry)
        put_scores(0, j + 1)
        return carry

    init = (jnp.full((1, W), MASK, F32), jnp.zeros((HEAD + ONES_ROWS, W), F32))
    last = qb // (GROUP // Q_BLOCK)
    quad_body = lambda jq, c: pair_body(2 * jq + 1, pair_body(2 * jq, c))
    oct_body = lambda jo, c: quad_body(2 * jo + 1, quad_body(2 * jo, c))
    carry = lax.fori_loop(0, last // 8, oct_body, init)
    carry = lax.fori_loop(2 * (last // 8), last // 4, quad_body, carry)
    carry = lax.fori_loop(2 * (last // 4), last // 2, pair_body, carry)
    carry = lax.fori_loop(2 * (last // 2), last, single_body, carry)

    k_row = lax.broadcasted_iota(jnp.int32, (KV_TILE, 1), 0)
    for i in range(tiles):
        tok = last * GROUP + i * KV_TILE + k_row
        causal = tile4(jnp.where(tok <= t_q, 0.0, MASK))
        s = sbuf_s[0, i * KV_TILE:(i + 1) * KV_TILE, :] + causal
        carry = online(s, vst_ref[last * tiles + i], carry)
    _, acc_sel = carry
    o_sel = acc_sel[:HEAD] * (1.0 / acc_sel[HEAD:HEAD + 1])
    o_win = owin_s[...]

    outs = []
    gates = g_ref[...]
    per_kv = GQA * 3
    gate = lambda r: jnp.where(hk == 0, gates[r:r + 1], gates[per_kv + r:per_kv + r + 1])
    for g in range(GQA):
        cols = slice(g * Q_BLOCK, (g + 1) * Q_BLOCK)
        o_g = (gate(g * 3) * o_cmp[:, cols] + gate(g * 3 + 1) * o_sel[:, cols]
               + gate(g * 3 + 2) * o_win[:, cols])
        outs.append(o_g)
    o_ref[...] = jnp.concatenate(outs, axis=0).T.astype(o_ref.dtype)


def _nsa(q, kc, vct, ks, vst, kw, vwt, gt, B, S):
    BS = q.shape[0]
    nqb = S // Q_BLOCK
    n_blk = S // CMP_STRIDE
    n_sel = S // SEL_BLOCK
    n_top = min(SEL_TOPK, n_sel)
    step = max(4, nqb // CMP_VARIANTS // 4 * 4)
    levels = tuple(range(step, nqb, step)) + (nqb,)
    per_head = lambda *blk: pl.BlockSpec((None, None) + blk, lambda b, h, i: (b, h) + (0,) * len(blk))
    qspec = pl.BlockSpec((Q_BLOCK, GQA * HEAD), lambda b, h, i: (b * nqb + i, h))
    return pl.pallas_call(
        functools.partial(_nsa_kernel, n_top=n_top, levels=levels),
        grid=(B, N_KV_HEADS, nqb),
        in_specs=[
            qspec,
            per_head(n_blk, HEAD),
            per_head(HEAD, n_blk),
            per_head(S, LANES),
            per_head(S // KV_TILE, HEAD + ONES_ROWS, KV_TILE),
            per_head(S, HEAD),
            per_head(S // LANES, HEAD, LANES),
            pl.BlockSpec((None, GATE_ROWS, Q_BLOCK), lambda b, h, i: (b, 0, i)),
        ],
        out_specs=qspec,
        out_shape=jax.ShapeDtypeStruct((BS, N_WIDTH), MXU_DTYPE),
        scratch_shapes=[
            pltpu.VMEM((n_sel, Q_BLOCK), F32),
            pltpu.VMEM((HEAD, GQA * Q_BLOCK), F32),
            pltpu.VMEM((HEAD, GQA * Q_BLOCK), F32),
            pltpu.VMEM((2, GROUP, GQA * Q_BLOCK), F32),
        ],
        compiler_params=pltpu.CompilerParams(
            dimension_semantics=("parallel", "parallel", "arbitrary"), vmem_limit_bytes=VMEM_LIMIT),
        name="nsa",
    )(q, kc, vct, ks, vst, kw, vwt, gt)


def _ffn_kernel(x_ref, yr_ref, yn_ref, wo_ref, nf_ref, wg_ref, wu_ref, wd_ref, nfin_ref, o_ref,
                h1_s, xn_s, acc_s):
    f = pl.program_id(1)
    tm = x_ref.shape[0]
    sub = min(tm, FFN_SUB)
    blocks = [slice(r0, r0 + sub) for r0 in range(0, tm, sub)]

    @pl.when(f == 0)
    def _():
        for rows in blocks:
            mix = _dot(yr_ref[rows, :], wo_ref[:R_WIDTH]) + _dot(yn_ref[rows, :], wo_ref[R_WIDTH:])
            h1 = x_ref[rows, :] + mix
            h1_s[rows, :] = h1
            xn_s[rows, :] = _rms(h1, nf_ref[...]).astype(xn_s.dtype)
        acc_s[...] = jnp.zeros_like(acc_s)

    for rows in blocks:
        xn = xn_s[rows, :]
        gate = _dot(xn, wg_ref[...])
        up = _dot(xn, wu_ref[...])
        acc_s[rows, :] += _dot(_mx(gate * jax.nn.sigmoid(gate) * up), wd_ref[...])

    @pl.when(f == pl.num_programs(1) - 1)
    def _():
        o_ref[...] = _rms(h1_s[...] + acc_s[...], nfin_ref[...])


def _ffn(x2, y_r, y_n, w_out, nf, w_gate, w_up, w_down, nfin, tm, tf):
    BS, D = x2.shape
    d_ff = w_gate.shape[1]
    row = lambda i, f: (i, 0)
    const = lambda i, f: (0, 0)
    return pl.pallas_call(
        _ffn_kernel,
        grid=(BS // tm, d_ff // tf),
        in_specs=[
            pl.BlockSpec((tm, D), row),
            pl.BlockSpec((tm, R_WIDTH), row),
            pl.BlockSpec((tm, N_WIDTH), row),
            pl.BlockSpec((R_WIDTH + N_WIDTH, D), const),
            pl.BlockSpec((1, D), const),
            pl.BlockSpec((D, tf), lambda i, f: (0, f)),
            pl.BlockSpec((D, tf), lambda i, f: (0, f)),
            pl.BlockSpec((tf, D), lambda i, f: (f, 0)),
            pl.BlockSpec((1, D), const),
        ],
        out_specs=pl.BlockSpec((tm, D), row),
        out_shape=jax.ShapeDtypeStruct((BS, D), F32),
        scratch_shapes=[pltpu.VMEM((tm, D), F32), pltpu.VMEM((tm, D), MXU_DTYPE), pltpu.VMEM((tm, D), F32)],
        compiler_params=pltpu.CompilerParams(
            dimension_semantics=("parallel", "arbitrary"), vmem_limit_bytes=VMEM_LIMIT),
        name="ffn",
    )(x2, y_r, y_n, w_out, nf, w_gate, w_up, w_down, nfin)


def _pad_to(a, n, axis):
    pad = [(0, 0)] * a.ndim
    pad[axis] = (0, n - a.shape[axis])
    return jnp.pad(a, pad)


def _rope_rows():
    lane = jnp.arange(LANES) % HEAD
    inv_freq = jnp.power(jnp.float32(ROPE_THETA), -jnp.arange(ROT_HALF, dtype=F32) * 2.0 / ROT_DIM)
    freq = jnp.where(lane < ROT_DIM, inv_freq[lane % ROT_HALF], 0.0).astype(F32)[None, :]
    sgn = jnp.where(lane < ROT_HALF, -1.0, jnp.where(lane < ROT_DIM, 1.0, 0.0)).astype(F32)[None, :]
    return freq, sgn


def _layer(h, positions, norm_mix, w_in, mu_shift, w0, w2, a0, a2, g2, k_k, k_a, r_k, lnx_w, lnx_b,
           pe_k, wk1, bk1, wk2, pe_v, wv1, bv1, wv2, w_out, norm_ffn, w_gate, w_up, w_down, norm_final):
    B, S, D = h.shape
    BS = B * S
    x2 = h.reshape(BS, D)
    freq, sgn = _rope_rows()
    row = lambda a: a.reshape(1, -1).astype(F32)

    w_r = _mx(_pad_to(w_in[:, :R_COLS], R_COLS_PAD, 1))
    w_n = _mx(_pad_to(w_in[:, R_COLS:], N_COLS_PAD, 1))
    tm = min(512, S)
    rw, q, kvc, ks, vst, kw, vwt, gt = _proj(
        x2, positions.reshape(BS, 1), row(norm_mix), freq, sgn, w_r, w_n, B, S, tm)

    T = min(256, S)
    head_of = jnp.arange(R_WIDTH) // HEAD
    gmat = (head_of[:, None] == head_of[None, :]).astype(MXU_DTYPE)
    tt = jnp.arange(T)
    ltri = ((tt[:, None] // CHUNK == tt[None, :] // CHUNK) & (tt[:, None] >= tt[None, :])).astype(MXU_DTYPE)
    w2p = _mx(_pad_to(w2, LANES, 0))
    a2p = _mx(jnp.concatenate([jnp.zeros((DECAY_LORA, R_WIDTH), F32), a2], axis=0))
    g2p = _mx(_pad_to(g2, GLAT_PAD, 0))
    y_r = _rwkv(rw, row(_pad_to(mu_shift, R_COLS_PAD, 0)), row(w0), w2p, row(a0), a2p, g2p,
                row(k_k), row(k_a), row(r_k), row(lnx_w), row(lnx_b), gmat, ltri, B, S, T)

    n_blk = S // CMP_STRIDE
    xblk = kvc.reshape(B, n_blk, CMP_STRIDE, 2 * N_KV_HEADS, HEAD).transpose(0, 3, 1, 2, 4)
    xblk = xblk.reshape(B, 2 * N_KV_HEADS, n_blk, CMP_STRIDE * HEAD)
    w1s = _mx(jnp.stack([wk1, wv1]))
    b1s = jnp.stack([bk1, bv1]).reshape(2, 1, CMP_HIDDEN).astype(F32)
    w2s = _mx(_pad_to(jnp.stack([wk2, wv2]), LANES, 2))
    pes = _pad_to(jnp.stack([pe_k, pe_v]).reshape(2, 1, CMP_BLOCK * HEAD), 8, 1).astype(F32)
    pos_c = _pad_to(positions[:, CMP_BLOCK - 1::CMP_STRIDE], n_blk, 1).reshape(B, n_blk, 1)
    cmp_rows, cmp_t = _compress(xblk, w1s, b1s, w2s, pes, pos_c, freq, sgn, B, n_blk)
    kc = cmp_rows[:, :N_KV_HEADS]
    vct = cmp_t[:, N_KV_HEADS:]

    y_n = _nsa(q, kc, vct, ks, vst, kw, vwt, gt, B, S)

    d_ff = w_gate.shape[1]
    tf = d_ff // 2 if (d_ff // 2) % LANES == 0 else d_ff
    out = _ffn(x2, y_r, y_n, _mx(w_out), row(norm_ffn), _mx(w_gate), _mx(w_up), _mx(w_down),
               row(norm_final), min(512, BS), tf)
    return out.reshape(B, S, D)


def kernel(x, positions, norm_mix, w_in, mu_shift, w0, w2, a0, a2, g2, k_k, k_a, r_k, lnx_w, lnx_b, pe_k, wk1, bk1, wk2, pe_v, wv1, bv1, wv2, w_out, norm_ffn, w_gate, w_up, w_down, norm_final):
    assert norm_mix.shape[0] == 1, "single-layer block"
    l = 0
    return _layer(x, positions, norm_mix[l], w_in[l], mu_shift[l], w0[l], w2[l], a0[l], a2[l], g2[l],
                  k_k[l], k_a[l], r_k[l].reshape(-1), lnx_w[l], lnx_b[l], pe_k[l], wk1[l], bk1[l], wk2[l],
                  pe_v[l], wv1[l], bv1[l], wv2[l], w_out[l], norm_ffn[l], w_gate[l], w_up[l], w_down[l],
                  norm_final)
```

```python
import functools

import jax
import jax.numpy as jnp
from jax import lax
from jax.experimental import pallas as pl
from jax.experimental.pallas import tpu as pltpu

F32 = jnp.float32
MXU_DTYPE = jnp.bfloat16

R_HEADS = 8
HEAD = 64
R_WIDTH = R_HEADS * HEAD
DECAY_LORA = 64
ICLR_LORA = 64
GATE_LORA = 160
LNX_EPS = 64e-5
R_COLS = 3 * R_WIDTH + DECAY_LORA + ICLR_LORA + GATE_LORA
N_Q_HEADS = 8
N_KV_HEADS = 2
GQA = N_Q_HEADS // N_KV_HEADS
N_WIDTH = N_Q_HEADS * HEAD
KV_WIDTH = N_KV_HEADS * HEAD
CMP_BLOCK = 32
CMP_STRIDE = 16
CMP_HIDDEN = 256
SEL_BLOCK = 64
SEL_TOPK = 16
WINDOW = 512
Q_BLOCK = 128
ROPE_THETA = 500000.0
ROT_DIM = HEAD // 4
ROT_HALF = ROT_DIM // 2
NORM_EPS = 1e-6
MASK = -1e30
FORCE = 1e6
REMOVED = -(2.0 ** 127)

LANES = 128
R_COLS_PAD = 1920
GLAT_PAD = 256
N_COLS_PAD = N_WIDTH + 6 * KV_WIDTH + LANES
GATE_ROWS = 32
CHUNK = 64
SLAB = 4 * HEAD
SEL_GROUP = 8
GROUP = SEL_GROUP * SEL_BLOCK
KV_TILE = 256
ONES_ROWS = 16
LOG2E = 1.4426950408889634
EXP_M05 = 0.6065306597126334
CMP_VARIANTS = 8
PROJ_SUB = 256
MIB = 1024 * 1024
VMEM_LIMIT = {"proj": 32 * MIB, "rwkv": 28 * MIB, "compress": 24 * MIB, "nsa": 44 * MIB, "ffn": 48 * MIB}


def _tiles(S, BS, d_ff):
    half_ff = d_ff // 2
    return dict(proj_rows=min(512, S), rwkv_rows=min(256, S), ffn_rows=min(512, BS),
                ffn_cols=half_ff if half_ff % LANES == 0 else d_ff)


def _dot(a, b):
    return jnp.dot(a, b, preferred_element_type=F32)


def _dot_nt(a, b):
    return lax.dot_general(a, b, (((1,), (1,)), ((), ())), preferred_element_type=F32)


def _mx(a):
    return a.astype(MXU_DTYPE)


def _rms(x, w):
    ms = jnp.mean(x * x, axis=-1, keepdims=True)
    return x * lax.rsqrt(ms + NORM_EPS) * w


def _rope_tables(pos_col, freq_row, sgn_row):
    ang = pos_col.astype(F32) * freq_row
    return jnp.cos(ang), jnp.sin(ang) * sgn_row


def _rope128(t, cos, sin, first):
    sh = jnp.where(first, pltpu.roll(t, LANES - ROT_HALF, 1), pltpu.roll(t, ROT_HALF, 1))
    return t * cos + sh * sin


def _proj_kernel(x_ref, pos_ref, nw_ref, freq_ref, sgn_ref, wr_ref, wn_ref,
                 rw_ref, q_ref, kvc_ref, ks_ref, vst_ref, kw_ref, vwt_ref, gt_ref):
    tm = x_ref.shape[0]
    sub = min(tm, PROJ_SUB)
    lane = lax.broadcasted_iota(jnp.int32, (1, LANES), 1)
    first = (lane % HEAD) < ROT_HALF
    ones_rows = jnp.where(lax.broadcasted_iota(jnp.int32, (ONES_ROWS, KV_TILE), 0) == 0, 1.0, 0.0)
    scale = HEAD ** -0.5 * LOG2E

    for r0 in range(0, tm, sub):
        rows = slice(r0, r0 + sub)
        xn = _mx(_rms(x_ref[rows, :], nw_ref[...]))
        rw_ref[rows, :] = _dot(xn, wr_ref[...])
        pn = _dot(xn, wn_ref[...])
        cos, sin = _rope_tables(pos_ref[rows, :], freq_ref[...], sgn_ref[...])
        rope = lambda t: _rope128(t, cos, sin, first)
        q = [rope(pn[:, i * LANES:(i + 1) * LANES]) * scale for i in range(N_WIDTH // LANES)]
        q_ref[rows, :] = jnp.concatenate(q, axis=1).astype(q_ref.dtype)
        o = N_WIDTH
        kvc_ref[rows, :] = pn[:, o:o + 2 * KV_WIDTH]
        o += 2 * KV_WIDTH

        row_g = pl.program_id(1) * tm + r0 + lax.broadcasted_iota(jnp.int32, (sub, 1), 0)
        onehot = jnp.where(lane == HEAD + (row_g // SEL_BLOCK) % SEL_GROUP, 1.0, 0.0)
        kk = rope(pn[:, o:o + KV_WIDTH])
        vt = rope(pn[:, o + KV_WIDTH:o + 2 * KV_WIDTH]).T
        o += 2 * KV_WIDTH
        for h in range(N_KV_HEADS):
            k_h = kk if h == 0 else pltpu.roll(kk, HEAD, 1)
            ks_ref[h, rows, :] = jnp.where(lane < HEAD, k_h, onehot).astype(ks_ref.dtype)
            for i in range(sub // KV_TILE):
                v_t = vt[h * HEAD:(h + 1) * HEAD, i * KV_TILE:(i + 1) * KV_TILE]
                vst_ref[h, r0 // KV_TILE + i] = jnp.concatenate([v_t, ones_rows], axis=0).astype(vst_ref.dtype)

        kk = rope(pn[:, o:o + KV_WIDTH])
        vt = rope(pn[:, o + KV_WIDTH:o + 2 * KV_WIDTH]).T
        o += 2 * KV_WIDTH
        for h in range(N_KV_HEADS):
            kw_ref[h, rows, :] = kk[:, h * HEAD:(h + 1) * HEAD].astype(kw_ref.dtype)
            for i in range(sub // LANES):
                vwt_ref[h, r0 // LANES + i] = vt[h * HEAD:(h + 1) * HEAD,
                                                 i * LANES:(i + 1) * LANES].astype(vwt_ref.dtype)
        gt_ref[:, rows] = jax.nn.sigmoid(pn[:, o:o + LANES]).T[:GATE_ROWS]


def _proj(x2, pos_col, nw, freq, sgn, w_r, w_n, B, S, tm):
    BS, D = x2.shape
    nt = S // tm
    nkt = S // LANES
    row = lambda b, i: (b * nt + i, 0)
    const = lambda b, i: (0, 0)
    head_rows = lambda width: pl.BlockSpec((None, N_KV_HEADS, tm, width), lambda b, i: (b, 0, i, 0))
    head_tiles = lambda rows, width: pl.BlockSpec(
        (None, N_KV_HEADS, tm // width, rows, width), lambda b, i: (b, 0, i, 0, 0))
    kv = lambda *shape: jax.ShapeDtypeStruct((B, N_KV_HEADS) + shape, MXU_DTYPE)
    return pl.pallas_call(
        _proj_kernel,
        grid=(B, nt),
        in_specs=[
            pl.BlockSpec((tm, D), row),
            pl.BlockSpec((tm, 1), row),
            pl.BlockSpec((1, D), const),
            pl.BlockSpec((1, LANES), const),
            pl.BlockSpec((1, LANES), const),
            pl.BlockSpec((D, R_COLS_PAD), const),
            pl.BlockSpec((D, N_COLS_PAD), const),
        ],
        out_specs=[
            pl.BlockSpec((tm, R_COLS_PAD), row),
            pl.BlockSpec((tm, N_WIDTH), row),
            pl.BlockSpec((tm, 2 * KV_WIDTH), row),
            head_rows(LANES), head_tiles(HEAD + ONES_ROWS, KV_TILE),
            head_rows(HEAD), head_tiles(HEAD, LANES),
            pl.BlockSpec((None, GATE_ROWS, tm), lambda b, i: (b, 0, i)),
        ],
        out_shape=[
            jax.ShapeDtypeStruct((BS, R_COLS_PAD), F32),
            jax.ShapeDtypeStruct((BS, N_WIDTH), MXU_DTYPE),
            jax.ShapeDtypeStruct((BS, 2 * KV_WIDTH), F32),
            kv(S, LANES), kv(S // KV_TILE, HEAD + ONES_ROWS, KV_TILE),
            kv(S, HEAD), kv(nkt, HEAD, LANES),
            jax.ShapeDtypeStruct((B, GATE_ROWS, S), F32),
        ],
        compiler_params=pltpu.CompilerParams(
            dimension_semantics=("parallel", "parallel"), vmem_limit_bytes=VMEM_LIMIT["proj"]),
        name="proj",
    )(x2, pos_col, nw, freq, sgn, w_r, w_n)


def _rwkv_kernel(rw_ref, mu_ref, w0_ref, w2_ref, a0_ref, a2_ref, g2_ref, kk_ref, ka_ref, rk_ref,
                 lw_ref, lb_ref, gmat_ref, ltri_ref, y_ref,
                 state_s, prev_s, at_s, rt_s, bt_s, kt_s, v_s, wc_s, y_s, bonus_s, g_s):
    B, T = rw_ref.shape[0], rw_ref.shape[1]

    @pl.when(pl.program_id(0) == 0)
    def _():
        state_s[...] = jnp.zeros_like(state_s)
        prev_s[...] = jnp.zeros_like(prev_s)

    gmat = gmat_ref[...]

    def gsum(t):
        return _dot(_mx(t), gmat)

    def prepare(b):
        rows = slice(b * T, (b + 1) * T)
        x = rw_ref[b]
        row = lax.broadcasted_iota(jnp.int32, (T, 1), 0)
        xs = jnp.where(row == 0, prev_s[b, 0:1, :], pltpu.roll(x, 1, 0))
        prev_s[b, 0:1, :] = x[T - 1:T, :]
        xm = x + (xs - x) * mu_ref[...]
        r = xm[:, 0:R_WIDTH]
        k = xm[:, R_WIDTH:2 * R_WIDTH]
        v = xm[:, 2 * R_WIDTH:3 * R_WIDTH]
        lat = xm[:, 3 * R_WIDTH:3 * R_WIDTH + LANES]
        glat = xm[:, 3 * R_WIDTH + LANES:3 * R_WIDTH + LANES + GLAT_PAD]

        wv = w0_ref[...] + _dot(_mx(jnp.tanh(lat)), w2_ref[...])
        ld = -EXP_M05 * jax.nn.sigmoid(wv)
        a = jax.nn.sigmoid(a0_ref[...] + _dot(_mx(lat), a2_ref[...]))
        g_s[rows, :] = _dot(_mx(jax.nn.sigmoid(glat)), g2_ref[...])
        kkv = k * kk_ref[...]
        kkn = kkv / jnp.maximum(jnp.sqrt(gsum(kkv * kkv)), 1e-12)
        k_mod = k * (1.0 + (a - 1.0) * ka_ref[...])
        bonus_s[rows, :] = gsum(r * k_mod * rk_ref[...]) * v

        ltri = ltri_ref[...]
        h1 = _mx(ld)
        h2 = _mx(ld - h1.astype(F32))
        cum = _dot(ltri, h1) + _dot(ltri, h2)
        w_inc = jnp.exp(cum)
        w_inv = jnp.exp(-cum)
        w_exc = jnp.exp(cum - ld)
        at_s[rows, :] = (-kkn * w_exc).astype(at_s.dtype)
        rt_s[rows, :] = (r * w_inc).astype(rt_s.dtype)
        bt_s[rows, :] = (kkn * a * w_inv).astype(bt_s.dtype)
        kt_s[rows, :] = (k_mod * w_inv).astype(kt_s.dtype)
        v_s[rows, :] = v
        wc_s[rows, :] = w_inc

    for b in range(B):
        prepare(b)

    rr = lax.broadcasted_iota(jnp.int32, (SLAB, SLAB), 0) // HEAD
    cc = lax.broadcasted_iota(jnp.int32, (SLAB, SLAB), 1) // HEAD
    diag_f = jnp.where(rr == cc, 1.0, 0.0)
    diag = _mx(diag_f)

    def bdiag(t):
        return jnp.concatenate([t] * (SLAB // HEAD), axis=0) * diag

    def bdiag_t(t):
        tt = jnp.concatenate([t, t], axis=0).T
        return _mx(jnp.concatenate([tt, tt], axis=1)) * diag

    ii = lax.broadcasted_iota(jnp.int32, (CHUNK, SLAB), 0)
    jj = lax.broadcasted_iota(jnp.int32, (CHUNK, SLAB), 1) % HEAD
    strict = ii > jj
    incl = ii >= jj
    eye = (ii == jj).astype(F32)
    chains = [(b, s) for b in range(B) for s in range(R_WIDTH // SLAB)]

    cs = range(len(chains))
    sl = [slice(s * SLAB, (s + 1) * SLAB) for _, s in chains]

    def state_free(c, out):
        rows = [slice(b * T + c * CHUNK, b * T + (c + 1) * CHUNK) for b, _ in chains]
        a_ = [at_s[rows[i], sl[i]] for i in cs]
        r_ = [rt_s[rows[i], sl[i]] for i in cs]
        b_ = [bt_s[rows[i], sl[i]] for i in cs]
        k_ = [kt_s[rows[i], sl[i]] for i in cs]
        vf = [v_s[rows[i], sl[i]] for i in cs]
        ar = [jnp.concatenate([a_[i], r_[i]], axis=0) for i in cs]
        out.update(rows=rows, vf=vf, ar=ar, bk=[jnp.concatenate([b_[i], k_[i]], axis=0) for i in cs])
        mb = [_dot(ar[i], bdiag_t(b_[i].astype(F32))) for i in cs]
        mk = [_dot(ar[i], bdiag_t(k_[i].astype(F32))) for i in cs]
        yield
        a_ab = [jnp.where(strict, mb[i][:CHUNK], 0.0) for i in cs]
        out["a_rb"] = [_mx(jnp.where(incl, mb[i][CHUNK:], 0.0)) for i in cs]
        a_k = [_mx(jnp.concatenate([jnp.where(strict, mk[i][:CHUNK], 0.0),
                                    jnp.where(incl, mk[i][CHUNK:], 0.0)], axis=0)) for i in cs]
        out["kv"] = [_dot(a_k[i], bdiag(_mx(vf[i]))) for i in cs]
        ab = [_mx(t) for t in a_ab]
        xp = [_dot(ab[i], bdiag(ab[i])) for i in cs]
        qs = [eye + a_ab[i] for i in cs]
        n = 2
        while n < CHUNK:
            yield
            xb = [bdiag(_mx(t)) for t in xp]
            if 2 * n < CHUNK:
                res = [_dot(_mx(jnp.concatenate([xp[i], qs[i]], axis=0)), xb[i]) for i in cs]
                xp = [res[i][:CHUNK] for i in cs]
                qs = [qs[i] + res[i][CHUNK:] for i in cs]
            else:
                upd = [_dot(_mx(qs[i]), xb[i]) for i in cs]
                qs = [qs[i] + upd[i] for i in cs]
            n *= 2
        out["inv"] = [_mx(t) for t in qs]

    def state_step(c, pre, state):
        ar, vf, rows = pre["ar"], pre["vf"], pre["rows"]
        asrs = [_dot(ar[i], bdiag_t(state[i])) for i in cs]
        yield
        rhs = [_mx(asrs[i][:CHUNK] + pre["kv"][i][:CHUNK]) for i in cs]
        u = [_dot(pre["inv"][i], bdiag(rhs[i])) for i in cs]
        yield
        yb = [_dot(pre["a_rb"][i], bdiag(_mx(u[i]))) for i in cs]
        uv_t = [_mx(jnp.concatenate([u[i], vf[i]], axis=0).T) for i in cs]
        full = [_dot(uv_t[i], pre["bk"][i]) * diag_f for i in cs]
        for i in cs:
            ds = full[i][0:HEAD]
            for h in range(1, SLAB // HEAD):
                ds = ds + full[i][h * HEAD:(h + 1) * HEAD]
            wc = wc_s[rows[i].stop - 1:rows[i].stop, sl[i]]
            state[i] = (state[i] + ds) * wc
            y_s[rows[i], sl[i]] = asrs[i][CHUNK:] + yb[i] + pre["kv"][i][CHUNK:]

    n_chunks = T // CHUNK
    state = [state_s[i] for i in cs]
    pre = {}
    for _ in state_free(0, pre):
        pass
    for c in range(n_chunks):
        nxt = {}
        streams = [state_step(c, pre, state)]
        if c + 1 < n_chunks:
            streams.append(state_free(c + 1, nxt))
        while streams:
            streams = [g for g in streams if next(g, streams) is not streams]
        pre = nxt
    for i in cs:
        state_s[i] = state[i]

    for b in range(B):
        rows = slice(b * T, (b + 1) * T)
        y = y_s[rows, :]
        mean = gsum(y) * (1.0 / HEAD)
        d = y - mean
        var = gsum(d * d) * (1.0 / HEAD)
        yn = d * lax.rsqrt(var + LNX_EPS) * lw_ref[...] + lb_ref[...]
        y_ref[b] = ((yn + bonus_s[rows, :]) * g_s[rows, :]).astype(y_ref.dtype)


def _rwkv(rw, mu, w0, w2p, a0, a2p, g2p, k_k, k_a, r_k, lnx_w, lnx_b, gmat, ltri, B, S, T):
    const = lambda i: (0, 0)
    vec = pl.BlockSpec((1, R_WIDTH), const)
    big = lambda dt: pltpu.VMEM((B * T, R_WIDTH), dt)
    y = pl.pallas_call(
        _rwkv_kernel,
        grid=(S // T,),
        in_specs=[
            pl.BlockSpec((B, T, R_COLS_PAD), lambda i: (0, i, 0)),
            pl.BlockSpec((1, R_COLS_PAD), const),
            vec,
            pl.BlockSpec((LANES, R_WIDTH), const),
            vec,
            pl.BlockSpec((LANES, R_WIDTH), const),
            pl.BlockSpec((GLAT_PAD, R_WIDTH), const),
            vec, vec, vec, vec, vec,
            pl.BlockSpec((R_WIDTH, R_WIDTH), const),
            pl.BlockSpec((T, T), const),
        ],
        out_specs=pl.BlockSpec((B, T, R_WIDTH), lambda i: (0, i, 0)),
        out_shape=jax.ShapeDtypeStruct((B, S, R_WIDTH), MXU_DTYPE),
        scratch_shapes=[
            pltpu.VMEM((B * (R_WIDTH // SLAB), HEAD, SLAB), F32),
            pltpu.VMEM((B, 8, R_COLS_PAD), F32),
            big(MXU_DTYPE), big(MXU_DTYPE), big(MXU_DTYPE), big(MXU_DTYPE),
            big(F32), big(F32), big(F32), big(F32), big(F32),
        ],
        compiler_params=pltpu.CompilerParams(
            dimension_semantics=("arbitrary",), vmem_limit_bytes=VMEM_LIMIT["rwkv"]),
        name="rwkv",
    )(rw.reshape(B, S, R_COLS_PAD), mu, w0, w2p, a0, a2p, g2p, k_k, k_a, r_k, lnx_w, lnx_b, gmat, ltri)
    return y.reshape(B * S, R_WIDTH)


def _cmp_kernel(x_ref, w1_ref, b1_ref, w2_ref, pe_ref, pos_ref, freq_ref, sgn_ref, o_ref, ot_ref):
    n_blk = x_ref.shape[0]
    half = x_ref.shape[1]
    j = pl.program_id(1)
    xb = _mx(x_ref[...])
    w1 = w1_ref[...]
    first_half = _dot(xb, w1[:half])
    second_half = pltpu.roll(_dot(xb, w1[half:]), n_blk - 1, 0)
    bias = _dot(_mx(pe_ref[...]), w1)[0:1] + b1_ref[...]
    hdn = first_half + second_half + bias
    hdn = hdn * jax.nn.sigmoid(hdn)
    out = _dot(_mx(hdn), w2_ref[...])
    cos, sin = _rope_tables(pos_ref[...], freq_ref[...], sgn_ref[...])
    lane = lax.broadcasted_iota(jnp.int32, (1, LANES), 1)
    roped = _rope128(out, cos, sin, (lane % HEAD) < ROT_HALF)
    res = jnp.where(j < N_KV_HEADS, roped, out)
    row = lax.broadcasted_iota(jnp.int32, (n_blk, 1), 0)
    res = jnp.where(row == n_blk - 1, 0.0, res)
    o_ref[...] = res[:, :HEAD].astype(o_ref.dtype)
    ot_ref[...] = res.T[:HEAD].astype(ot_ref.dtype)


def _compress(xblk, w1s, b1s, w2s, pes, pos_c, freq, sgn, B, n_blk):
    half = xblk.shape[-1]
    return pl.pallas_call(
        _cmp_kernel,
        grid=(B, 2 * N_KV_HEADS),
        in_specs=[
            pl.BlockSpec((None, None, n_blk, half), lambda b, j: (b, j, 0, 0)),
            pl.BlockSpec((None, 2 * half, CMP_HIDDEN), lambda b, j: (j // N_KV_HEADS, 0, 0)),
            pl.BlockSpec((None, 1, CMP_HIDDEN), lambda b, j: (j // N_KV_HEADS, 0, 0)),
            pl.BlockSpec((None, CMP_HIDDEN, LANES), lambda b, j: (j // N_KV_HEADS, 0, 0)),
            pl.BlockSpec((None, 8, 2 * half), lambda b, j: (j // N_KV_HEADS, 0, 0)),
            pl.BlockSpec((None, n_blk, 1), lambda b, j: (b, 0, 0)),
            pl.BlockSpec((1, LANES), lambda b, j: (0, 0)),
            pl.BlockSpec((1, LANES), lambda b, j: (0, 0)),
        ],
        out_specs=[
            pl.BlockSpec((None, None, n_blk, HEAD), lambda b, j: (b, j, 0, 0)),
            pl.BlockSpec((None, None, HEAD, n_blk), lambda b, j: (b, j, 0, 0)),
        ],
        out_shape=[
            jax.ShapeDtypeStruct((B, 2 * N_KV_HEADS, n_blk, HEAD), MXU_DTYPE),
            jax.ShapeDtypeStruct((B, 2 * N_KV_HEADS, HEAD, n_blk), MXU_DTYPE),
        ],
        compiler_params=pltpu.CompilerParams(
            dimension_semantics=("parallel", "parallel"), vmem_limit_bytes=VMEM_LIMIT["compress"]),
        name="compress",
    )(xblk, w1s, b1s, w2s, pes, pos_c, freq, sgn)


def _nsa_kernel(q_ref, kc_ref, vct_ref, ks_ref, vst_ref, kw_ref, vwt_ref, g_ref, o_ref,
                sel_s, ocmp_s, owin_s, sbuf_s, *, n_top, levels):
    n_blk = kc_ref.shape[0]
    n_sel = sel_s.shape[0]
    hk = pl.program_id(1)
    qb = pl.program_id(2)
    start = qb * Q_BLOCK
    W = GQA * Q_BLOCK

    q_t = q_ref[...].astype(F32).T
    qt = _mx(jnp.concatenate([q_t[g * HEAD:(g + 1) * HEAD] for g in range(GQA)], axis=1))
    tile4 = lambda t: jnp.concatenate([t] * GQA, axis=1)
    qi = lax.broadcasted_iota(jnp.int32, (1, Q_BLOCK), 1)
    t_q = start + qi
    t_w = tile4(t_q)

    tiles = GROUP // KV_TILE

    def compressed_and_select(n_c, n_s):
        w0 = pl.multiple_of(jnp.maximum(start - WINDOW, 0), LANES)
        n_win = WINDOW + Q_BLOCK
        s_win = _dot(kw_ref[pl.ds(w0, n_win), :], qt)

        c_end = lax.broadcasted_iota(jnp.int32, (n_c, 1), 0) * CMP_STRIDE + (CMP_BLOCK - 1)
        sc = _dot(kc_ref[0:n_c, :], qt) + tile4(jnp.where(c_end <= t_q, 0.0, MASK))
        q_plain = jnp.concatenate([qt, jnp.zeros((LANES - HEAD, W), MXU_DTYPE)], axis=0)
        s_first = [_dot(ks_ref[i * KV_TILE:(i + 1) * KV_TILE, :], q_plain) for i in range(tiles)]

        tok = w0 + lax.broadcasted_iota(jnp.int32, (n_win, 1), 0)
        in_win = (tok <= t_q) & (tok > t_q - WINDOW)
        s_win = s_win + tile4(jnp.where(in_win, 0.0, MASK))
        p_win = jnp.exp2(s_win - jnp.max(s_win, axis=0, keepdims=True))
        l_win = jnp.sum(p_win, axis=0, keepdims=True)
        p_win = _mx(p_win)
        acc_win = jnp.zeros((HEAD, W), F32)
        for i in range(n_win // LANES):
            acc_win = acc_win + _dot(vwt_ref[w0 // LANES + i], p_win[i * LANES:(i + 1) * LANES])
        owin_s[...] = acc_win * (1.0 / l_win)

        m = jnp.max(sc, axis=0, keepdims=True)
        p = jnp.exp2(sc - m)
        l = jnp.sum(p, axis=0, keepdims=True)
        pn = p * jnp.where(t_w >= CMP_BLOCK - 1, 1.0 / l, 0.0)
        ocmp_s[...] = _dot(vct_ref[:, 0:n_c], _mx(pn))

        ps = pn[:, 0:Q_BLOCK]
        for g in range(1, GQA):
            ps = ps + pn[:, g * Q_BLOCK:(g + 1) * Q_BLOCK]
        hi = _mx(ps)
        lo = _mx(ps - hi.astype(F32))
        j_col = lax.broadcasted_iota(jnp.int32, (n_s, 1), 0)
        c_row = lax.broadcasted_iota(jnp.int32, (1, n_c), 1)
        cover = ((c_row * CMP_STRIDE < j_col * SEL_BLOCK + SEL_BLOCK)
                 & (c_row * CMP_STRIDE + CMP_BLOCK > j_col * SEL_BLOCK))
        cover = jnp.where(cover, 1.0, 0.0).astype(MXU_DTYPE)
        imp = _dot(cover, hi) + _dot(cover, lo)

        cur = t_q // SEL_BLOCK
        forced = (j_col == 0) | (j_col == cur) | (j_col == cur - 1)
        score = jnp.where(forced, REMOVED, jnp.where(j_col <= cur, imp, MASK))
        for _ in range(max(n_top - 3, 0)):
            mx = jnp.max(score, axis=0, keepdims=True)
            idx = jnp.min(jnp.where(score == mx, j_col, n_s), axis=0, keepdims=True)
            score = jnp.where(j_col == idx, REMOVED, score)
        sel = jnp.where(score == REMOVED, 1.0, 0.0)
        sel_s[0:n_s, :] = sel

        neg = jnp.where(sel[0:SEL_GROUP] > 0.5, 0.0, MASK)
        per_tile = KV_TILE // SEL_BLOCK
        for i in range(tiles):
            bias = jnp.concatenate([jnp.broadcast_to(neg[r:r + 1], (SEL_BLOCK, Q_BLOCK))
                                    for r in range(i * per_tile, (i + 1) * per_tile)], axis=0)
            sbuf_s[0, i * KV_TILE:(i + 1) * KV_TILE, :] = s_first[i] + tile4(bias)

    lo_qb = 0
    for hi_qb in levels:
        n_c = min(hi_qb * (Q_BLOCK // CMP_STRIDE), n_blk)
        n_s = min(hi_qb * (Q_BLOCK // SEL_BLOCK), n_sel)
        pl.when((qb >= lo_qb) & (qb < hi_qb))(functools.partial(compressed_and_select, n_c, n_s))
        lo_qb = hi_qb
    o_cmp = ocmp_s[...]

    zero_rows = jnp.zeros((LANES - HEAD - 2 * SEL_GROUP, W), MXU_DTYPE)

    def group_scores(j):
        grp = sel_s[pl.ds(pl.multiple_of(j * SEL_GROUP, SEL_GROUP), SEL_GROUP), :]
        neg = jnp.where(grp > 0.5, 0.0, MASK)
        neg = tile4(jnp.concatenate([neg, jnp.zeros_like(neg)], axis=0))
        q_aug = jnp.concatenate([qt, _mx(neg), zero_rows], axis=0)
        r0 = pl.multiple_of(j * GROUP, GROUP)
        return [_dot(ks_ref[pl.ds(r0 + i * KV_TILE, KV_TILE), :], q_aug) for i in range(tiles)]

    def put_scores(slot, j):
        for i, s in enumerate(group_scores(j)):
            sbuf_s[slot, i * KV_TILE:(i + 1) * KV_TILE, :] = s

    def online(s, vt_tile, carry):
        m_, acc = carry
        m_new = jnp.maximum(m_, jnp.max(s, axis=0, keepdims=True))
        alpha = jnp.exp2(m_ - m_new)
        return m_new, alpha * acc + _dot(vt_tile, _mx(jnp.exp2(s - m_new)))

    def consume(slot, j, carry):
        for i in range(tiles):
            s = sbuf_s[slot, i * KV_TILE:(i + 1) * KV_TILE, :]
            carry = online(s, vst_ref[j * tiles + i], carry)
        return carry

    def pair_body(jp, carry):
        j = 2 * jp
        put_scores(1, j + 1)
        carry = consume(0, j, carry)
        put_scores(0, j + 2)
        return consume(1, j + 1, carry)

    def single_body(j, carry):
        carry = consume(0, j, carry)
        put_scores(0, j + 1)
        return carry

    init = (jnp.full((1, W), MASK, F32), jnp.zeros((HEAD + ONES_ROWS, W), F32))
    last = qb // (GROUP // Q_BLOCK)
    quad_body = lambda jq, c: pair_body(2 * jq + 1, pair_body(2 * jq, c))
    oct_body = lambda jo, c: quad_body(2 * jo + 1, quad_body(2 * jo, c))
    carry = lax.fori_loop(0, last // 8, oct_body, init)
    carry = lax.fori_loop(2 * (last // 8), last // 4, quad_body, carry)
    carry = lax.fori_loop(2 * (last // 4), last // 2, pair_body, carry)
    carry = lax.fori_loop(2 * (last // 2), last, single_body, carry)

    k_row = lax.broadcasted_iota(jnp.int32, (KV_TILE, 1), 0)
    for i in range(tiles):
        tok = last * GROUP + i * KV_TILE + k_row
        causal = tile4(jnp.where(tok <= t_q, 0.0, MASK))
        s = sbuf_s[0, i * KV_TILE:(i + 1) * KV_TILE, :] + causal
        carry = online(s, vst_ref[last * tiles + i], carry)
    _, acc_sel = carry
    o_sel = acc_sel[:HEAD] * (1.0 / acc_sel[HEAD:HEAD + 1])
    o_win = owin_s[...]

    outs = []
    gates = g_ref[...]
    per_kv = GQA * 3
    gate = lambda r: jnp.where(hk == 0, gates[r:r + 1], gates[per_kv + r:per_kv + r + 1])
    for g in range(GQA):
        cols = slice(g * Q_BLOCK, (g + 1) * Q_BLOCK)
        o_g = (gate(g * 3) * o_cmp[:, cols] + gate(g * 3 + 1) * o_sel[:, cols]
               + gate(g * 3 + 2) * o_win[:, cols])
        outs.append(o_g)
    o_ref[...] = jnp.concatenate(outs, axis=0).T.astype(o_ref.dtype)


def _nsa(q, kc, vct, ks, vst, kw, vwt, gt, B, S):
    BS = q.shape[0]
    nqb = S // Q_BLOCK
    n_blk = S // CMP_STRIDE
    n_sel = S // SEL_BLOCK
    n_top = min(SEL_TOPK, n_sel)
    step = max(4, nqb // CMP_VARIANTS // 4 * 4)
    levels = tuple(range(step, nqb, step)) + (nqb,)
    per_head = lambda *blk: pl.BlockSpec((None, None) + blk, lambda b, h, i: (b, h) + (0,) * len(blk))
    qspec = pl.BlockSpec((Q_BLOCK, GQA * HEAD), lambda b, h, i: (b * nqb + i, h))
    return pl.pallas_call(
        functools.partial(_nsa_kernel, n_top=n_top, levels=levels),
        grid=(B, N_KV_HEADS, nqb),
        in_specs=[
            qspec,
            per_head(n_blk, HEAD),
            per_head(HEAD, n_blk),
            per_head(S, LANES),
            per_head(S // KV_TILE, HEAD + ONES_ROWS, KV_TILE),
            per_head(S, HEAD),
            per_head(S // LANES, HEAD, LANES),
            pl.BlockSpec((None, GATE_ROWS, Q_BLOCK), lambda b, h, i: (b, 0, i)),
        ],
        out_specs=qspec,
        out_shape=jax.ShapeDtypeStruct((BS, N_WIDTH), MXU_DTYPE),
        scratch_shapes=[
            pltpu.VMEM((n_sel, Q_BLOCK), F32),
            pltpu.VMEM((HEAD, GQA * Q_BLOCK), F32),
            pltpu.VMEM((HEAD, GQA * Q_BLOCK), F32),
            pltpu.VMEM((2, GROUP, GQA * Q_BLOCK), F32),
        ],
        compiler_params=pltpu.CompilerParams(
            dimension_semantics=("parallel", "parallel", "arbitrary"), vmem_limit_bytes=VMEM_LIMIT["nsa"]),
        name="nsa",
    )(q, kc, vct, ks, vst, kw, vwt, gt)


def _ffn_kernel(x_ref, yr_ref, yn_ref, wo_ref, nf_ref, wg_ref, wu_ref, wd_ref, nfin_ref, o_ref,
                h1_s, xn_s, acc_s):
    f = pl.program_id(1)

    @pl.when(f == 0)
    def _():
        mix = _dot(yr_ref[...], wo_ref[:R_WIDTH]) + _dot(yn_ref[...], wo_ref[R_WIDTH:])
        h1 = x_ref[...] + mix
        h1_s[...] = h1
        xn_s[...] = _rms(h1, nf_ref[...]).astype(xn_s.dtype)
        acc_s[...] = jnp.zeros_like(acc_s)

    xn = xn_s[...]
    gate = _dot(xn, wg_ref[...])
    up = _dot(xn, wu_ref[...])
    acc_s[...] += _dot(_mx(gate * jax.nn.sigmoid(gate) * up), wd_ref[...])

    @pl.when(f == pl.num_programs(1) - 1)
    def _():
        o_ref[...] = _rms(h1_s[...] + acc_s[...], nfin_ref[...])


def _ffn(x2, y_r, y_n, w_out, nf, w_gate, w_up, w_down, nfin, tm, tf):
    BS, D = x2.shape
    d_ff = w_gate.shape[1]
    row = lambda i, f: (i, 0)
    const = lambda i, f: (0, 0)
    return pl.pallas_call(
        _ffn_kernel,
        grid=(BS // tm, d_ff // tf),
        in_specs=[
            pl.BlockSpec((tm, D), row),
            pl.BlockSpec((tm, R_WIDTH), row),
            pl.BlockSpec((tm, N_WIDTH), row),
            pl.BlockSpec((R_WIDTH + N_WIDTH, D), const),
            pl.BlockSpec((1, D), const),
            pl.BlockSpec((D, tf), lambda i, f: (0, f)),
            pl.BlockSpec((D, tf), lambda i, f: (0, f)),
            pl.BlockSpec((tf, D), lambda i, f: (f, 0)),
            pl.BlockSpec((1, D), const),
        ],
        out_specs=pl.BlockSpec((tm, D), row),
        out_shape=jax.ShapeDtypeStruct((BS, D), F32),
        scratch_shapes=[pltpu.VMEM((tm, D), F32), pltpu.VMEM((tm, D), MXU_DTYPE), pltpu.VMEM((tm, D), F32)],
        compiler_params=pltpu.CompilerParams(
            dimension_semantics=("parallel", "arbitrary"), vmem_limit_bytes=VMEM_LIMIT["ffn"]),
        name="ffn",
    )(x2, y_r, y_n, w_out, nf, w_gate, w_up, w_down, nfin)


def _pad_to(a, n, axis):
    pad = [(0, 0)] * a.ndim
    pad[axis] = (0, n - a.shape[axis])
    return jnp.pad(a, pad)


def _rope_rows():
    lane = jnp.arange(LANES) % HEAD
    inv_freq = jnp.power(jnp.float32(ROPE_THETA), -jnp.arange(ROT_HALF, dtype=F32) * 2.0 / ROT_DIM)
    freq = jnp.where(lane < ROT_DIM, inv_freq[lane % ROT_HALF], 0.0).astype(F32)[None, :]
    sgn = jnp.where(lane < ROT_HALF, -1.0, jnp.where(lane < ROT_DIM, 1.0, 0.0)).astype(F32)[None, :]
    return freq, sgn


def _layer(h, positions, norm_mix, w_in, mu_shift, w0, w2, a0, a2, g2, k_k, k_a, r_k, lnx_w, lnx_b,
           pe_k, wk1, bk1, wk2, pe_v, wv1, bv1, wv2, w_out, norm_ffn, w_gate, w_up, w_down, norm_final):
    B, S, D = h.shape
    BS = B * S
    x2 = h.reshape(BS, D)
    freq, sgn = _rope_rows()
    row = lambda a: a.reshape(1, -1).astype(F32)

    w_r = _mx(_pad_to(w_in[:, :R_COLS], R_COLS_PAD, 1))
    w_n = _mx(_pad_to(w_in[:, R_COLS:], N_COLS_PAD, 1))
    tiles = _tiles(S, BS, w_gate.shape[1])
    rw, q, kvc, ks, vst, kw, vwt, gt = _proj(
        x2, positions.reshape(BS, 1), row(norm_mix), freq, sgn, w_r, w_n, B, S, tiles["proj_rows"])

    T = tiles["rwkv_rows"]
    head_of = jnp.arange(R_WIDTH) // HEAD
    gmat = (head_of[:, None] == head_of[None, :]).astype(MXU_DTYPE)
    tt = jnp.arange(T)
    ltri = ((tt[:, None] // CHUNK == tt[None, :] // CHUNK) & (tt[:, None] >= tt[None, :])).astype(MXU_DTYPE)
    w2p = _mx(_pad_to(w2, LANES, 0))
    a2p = _mx(jnp.concatenate([jnp.zeros((DECAY_LORA, R_WIDTH), F32), a2], axis=0))
    g2p = _mx(_pad_to(g2, GLAT_PAD, 0))
    y_r = _rwkv(rw, row(_pad_to(mu_shift, R_COLS_PAD, 0)), row(w0), w2p, row(a0), a2p, g2p,
                row(k_k), row(k_a), row(r_k), row(lnx_w), row(lnx_b), gmat, ltri, B, S, T)

    n_blk = S // CMP_STRIDE
    xblk = kvc.reshape(B, n_blk, CMP_STRIDE, 2 * N_KV_HEADS, HEAD).transpose(0, 3, 1, 2, 4)
    xblk = xblk.reshape(B, 2 * N_KV_HEADS, n_blk, CMP_STRIDE * HEAD)
    w1s = _mx(jnp.stack([wk1, wv1]))
    b1s = jnp.stack([bk1, bv1]).reshape(2, 1, CMP_HIDDEN).astype(F32)
    w2s = _mx(_pad_to(jnp.stack([wk2, wv2]), LANES, 2))
    pes = _pad_to(jnp.stack([pe_k, pe_v]).reshape(2, 1, CMP_BLOCK * HEAD), 8, 1).astype(F32)
    pos_c = _pad_to(positions[:, CMP_BLOCK - 1::CMP_STRIDE], n_blk, 1).reshape(B, n_blk, 1)
    cmp_rows, cmp_t = _compress(xblk, w1s, b1s, w2s, pes, pos_c, freq, sgn, B, n_blk)
    kc = cmp_rows[:, :N_KV_HEADS]
    vct = cmp_t[:, N_KV_HEADS:]

    y_n = _nsa(q, kc, vct, ks, vst, kw, vwt, gt, B, S)

    out = _ffn(x2, y_r, y_n, _mx(w_out), row(norm_ffn), _mx(w_gate), _mx(w_up), _mx(w_down),
               row(norm_final), tiles["ffn_rows"], tiles["ffn_cols"])
    return out.reshape(B, S, D)


def kernel(x, positions, norm_mix, w_in, mu_shift, w0, w2, a0, a2, g2, k_k, k_a, r_k, lnx_w, lnx_b, pe_k, wk1, bk1, wk2, pe_v, wv1, bv1, wv2, w_out, norm_ffn, w_gate, w_up, w_down, norm_final):
    assert norm_mix.shape[0] == 1, "single-layer block"
    l = 0
    return _layer(x, positions, norm_mix[l], w_in[l], mu_shift[l], w0[l], w2[l], a0[l], a2[l], g2[l],
                  k_k[l], k_a[l], r_k[l].reshape(-1), lnx_w[l], lnx_b[l], pe_k[l], wk1[l], bk1[l], wk2[l],
                  pe_v[l], wv1[l], bv1[l], wv2[l], w_out[l], norm_ffn[l], w_gate[l], w_up[l], w_down[l],
                  norm_final)
```

```python
import functools

import jax
import jax.numpy as jnp
from jax import lax
from jax.experimental import pallas as pl
from jax.experimental.pallas import tpu as pltpu

F32 = jnp.float32
MXU_DTYPE = jnp.bfloat16

R_HEADS = 8
HEAD = 64
R_WIDTH = R_HEADS * HEAD
DECAY_LORA = 64
ICLR_LORA = 64
GATE_LORA = 160
LNX_EPS = 64e-5
R_COLS = 3 * R_WIDTH + DECAY_LORA + ICLR_LORA + GATE_LORA
N_Q_HEADS = 8
N_KV_HEADS = 2
GQA = N_Q_HEADS // N_KV_HEADS
N_WIDTH = N_Q_HEADS * HEAD
KV_WIDTH = N_KV_HEADS * HEAD
CMP_BLOCK = 32
CMP_STRIDE = 16
CMP_HIDDEN = 256
SEL_BLOCK = 64
SEL_TOPK = 16
WINDOW = 512
Q_BLOCK = 128
ROPE_THETA = 500000.0
ROT_DIM = HEAD // 4
ROT_HALF = ROT_DIM // 2
NORM_EPS = 1e-6
MASK = -1e30
FORCE = 1e6
REMOVED = -(2.0 ** 127)

LANES = 128
R_COLS_PAD = 1920
GLAT_PAD = 256
N_COLS_PAD = N_WIDTH + 6 * KV_WIDTH + LANES
GATE_ROWS = 32
CHUNK = 64
SLAB = 4 * HEAD
SEL_GROUP = 8
GROUP = SEL_GROUP * SEL_BLOCK
KV_TILE = 256
ONES_ROWS = 16
LOG2E = 1.4426950408889634
EXP_M05 = 0.6065306597126334
CMP_VARIANTS = 8
PROJ_SUB = 256
MIB = 1024 * 1024
VMEM_LIMIT = {"proj": 32 * MIB, "rwkv": 28 * MIB, "compress": 24 * MIB, "nsa": 44 * MIB, "ffn": 48 * MIB}


def _tiles(S, BS, d_ff):
    half_ff = d_ff // 2
    return dict(proj_rows=min(512, S), rwkv_rows=min(256, S), ffn_rows=min(512, BS),
                ffn_cols=half_ff if half_ff % LANES == 0 else d_ff)


def _dot(a, b):
    return jnp.dot(a, b, preferred_element_type=F32)


def _dot_nt(a, b):
    return lax.dot_general(a, b, (((1,), (1,)), ((), ())), preferred_element_type=F32)


def _mx(a):
    return a.astype(MXU_DTYPE)


def _rms(x, w):
    ms = jnp.mean(x * x, axis=-1, keepdims=True)
    return x * lax.rsqrt(ms + NORM_EPS) * w


def _rope_tables(pos_col, freq_row, sgn_row):
    ang = pos_col.astype(F32) * freq_row
    return jnp.cos(ang), jnp.sin(ang) * sgn_row


def _rope128(t, cos, sin, first):
    sh = jnp.where(first, pltpu.roll(t, LANES - ROT_HALF, 1), pltpu.roll(t, ROT_HALF, 1))
    return t * cos + sh * sin


def _proj_kernel(x_ref, pos_ref, nw_ref, freq_ref, sgn_ref, wr_ref, wn_ref,
                 rw_ref, q_ref, kvc_ref, ks_ref, vst_ref, kw_ref, vwt_ref, gt_ref):
    tm = x_ref.shape[0]
    sub = min(tm, PROJ_SUB)
    lane = lax.broadcasted_iota(jnp.int32, (1, LANES), 1)
    first = (lane % HEAD) < ROT_HALF
    ones_rows = jnp.where(lax.broadcasted_iota(jnp.int32, (ONES_ROWS, KV_TILE), 0) == 0, 1.0, 0.0)
    scale = HEAD ** -0.5 * LOG2E

    for r0 in range(0, tm, sub):
        rows = slice(r0, r0 + sub)
        xn = _mx(_rms(x_ref[rows, :], nw_ref[...]))
        rw_ref[rows, :] = _dot(xn, wr_ref[...])
        pn = _dot(xn, wn_ref[...])
        cos, sin = _rope_tables(pos_ref[rows, :], freq_ref[...], sgn_ref[...])
        rope = lambda t: _rope128(t, cos, sin, first)
        q = [rope(pn[:, i * LANES:(i + 1) * LANES]) * scale for i in range(N_WIDTH // LANES)]
        q_ref[rows, :] = jnp.concatenate(q, axis=1).astype(q_ref.dtype)
        o = N_WIDTH
        for j in range(2 * N_KV_HEADS):
            kvc_ref[j, rows, :] = pn[:, o + j * HEAD:o + (j + 1) * HEAD]
        o += 2 * KV_WIDTH

        row_g = pl.program_id(1) * tm + r0 + lax.broadcasted_iota(jnp.int32, (sub, 1), 0)
        onehot = jnp.where(lane == HEAD + (row_g // SEL_BLOCK) % SEL_GROUP, 1.0, 0.0)
        kk = rope(pn[:, o:o + KV_WIDTH])
        vt = rope(pn[:, o + KV_WIDTH:o + 2 * KV_WIDTH]).T
        o += 2 * KV_WIDTH
        for h in range(N_KV_HEADS):
            k_h = kk if h == 0 else pltpu.roll(kk, HEAD, 1)
            ks_ref[h, rows, :] = jnp.where(lane < HEAD, k_h, onehot).astype(ks_ref.dtype)
            for i in range(sub // KV_TILE):
                v_t = vt[h * HEAD:(h + 1) * HEAD, i * KV_TILE:(i + 1) * KV_TILE]
                vst_ref[h, r0 // KV_TILE + i] = jnp.concatenate([v_t, ones_rows], axis=0).astype(vst_ref.dtype)

        kk = rope(pn[:, o:o + KV_WIDTH])
        vt = rope(pn[:, o + KV_WIDTH:o + 2 * KV_WIDTH]).T
        o += 2 * KV_WIDTH
        for h in range(N_KV_HEADS):
            kw_ref[h, rows, :] = kk[:, h * HEAD:(h + 1) * HEAD].astype(kw_ref.dtype)
            for i in range(sub // LANES):
                vwt_ref[h, r0 // LANES + i] = vt[h * HEAD:(h + 1) * HEAD,
                                                 i * LANES:(i + 1) * LANES].astype(vwt_ref.dtype)
        gt_ref[:, rows] = jax.nn.sigmoid(pn[:, o:o + LANES]).T[:GATE_ROWS]


def _proj(x2, pos_col, nw, freq, sgn, w_r, w_n, B, S, tm):
    BS, D = x2.shape
    nt = S // tm
    nkt = S // LANES
    row = lambda b, i: (b * nt + i, 0)
    const = lambda b, i: (0, 0)
    head_rows = lambda width: pl.BlockSpec((None, N_KV_HEADS, tm, width), lambda b, i: (b, 0, i, 0))
    head_tiles = lambda rows, width: pl.BlockSpec(
        (None, N_KV_HEADS, tm // width, rows, width), lambda b, i: (b, 0, i, 0, 0))
    kv = lambda *shape: jax.ShapeDtypeStruct((B, N_KV_HEADS) + shape, MXU_DTYPE)
    return pl.pallas_call(
        _proj_kernel,
        grid=(B, nt),
        in_specs=[
            pl.BlockSpec((tm, D), row),
            pl.BlockSpec((tm, 1), row),
            pl.BlockSpec((1, D), const),
            pl.BlockSpec((1, LANES), const),
            pl.BlockSpec((1, LANES), const),
            pl.BlockSpec((D, R_COLS_PAD), const),
            pl.BlockSpec((D, N_COLS_PAD), const),
        ],
        out_specs=[
            pl.BlockSpec((tm, R_COLS_PAD), row),
            pl.BlockSpec((tm, N_WIDTH), row),
            pl.BlockSpec((2 * N_KV_HEADS, tm, HEAD), lambda b, i: (0, b * nt + i, 0)),
            head_rows(LANES), head_tiles(HEAD + ONES_ROWS, KV_TILE),
            head_rows(HEAD), head_tiles(HEAD, LANES),
            pl.BlockSpec((None, GATE_ROWS, tm), lambda b, i: (b, 0, i)),
        ],
        out_shape=[
            jax.ShapeDtypeStruct((BS, R_COLS_PAD), F32),
            jax.ShapeDtypeStruct((BS, N_WIDTH), MXU_DTYPE),
            jax.ShapeDtypeStruct((2 * N_KV_HEADS, BS, HEAD), F32),
            kv(S, LANES), kv(S // KV_TILE, HEAD + ONES_ROWS, KV_TILE),
            kv(S, HEAD), kv(nkt, HEAD, LANES),
            jax.ShapeDtypeStruct((B, GATE_ROWS, S), F32),
        ],
        compiler_params=pltpu.CompilerParams(
            dimension_semantics=("parallel", "parallel"), vmem_limit_bytes=VMEM_LIMIT["proj"]),
        name="proj",
    )(x2, pos_col, nw, freq, sgn, w_r, w_n)


def _rwkv_kernel(rw_ref, mu_ref, w0_ref, w2_ref, a0_ref, a2_ref, g2_ref, kk_ref, ka_ref, rk_ref,
                 lw_ref, lb_ref, gmat_ref, ltri_ref, y_ref,
                 state_s, prev_s, at_s, rt_s, bt_s, kt_s, v_s, wc_s, y_s, bonus_s, g_s):
    B, T = rw_ref.shape[0], rw_ref.shape[1]

    @pl.when(pl.program_id(0) == 0)
    def _():
        state_s[...] = jnp.zeros_like(state_s)
        prev_s[...] = jnp.zeros_like(prev_s)

    gmat = gmat_ref[...]

    def gsum(t):
        return _dot(_mx(t), gmat)

    def prepare(b):
        rows = slice(b * T, (b + 1) * T)
        x = rw_ref[b]
        row = lax.broadcasted_iota(jnp.int32, (T, 1), 0)
        xs = jnp.where(row == 0, prev_s[b, 0:1, :], pltpu.roll(x, 1, 0))
        prev_s[b, 0:1, :] = x[T - 1:T, :]
        xm = x + (xs - x) * mu_ref[...]
        r = xm[:, 0:R_WIDTH]
        k = xm[:, R_WIDTH:2 * R_WIDTH]
        v = xm[:, 2 * R_WIDTH:3 * R_WIDTH]
        lat = xm[:, 3 * R_WIDTH:3 * R_WIDTH + LANES]
        glat = xm[:, 3 * R_WIDTH + LANES:3 * R_WIDTH + LANES + GLAT_PAD]

        wv = w0_ref[...] + _dot(_mx(jnp.tanh(lat)), w2_ref[...])
        ld = -EXP_M05 * jax.nn.sigmoid(wv)
        a = jax.nn.sigmoid(a0_ref[...] + _dot(_mx(lat), a2_ref[...]))
        g_s[rows, :] = _dot(_mx(jax.nn.sigmoid(glat)), g2_ref[...])
        kkv = k * kk_ref[...]
        kkn = kkv / jnp.maximum(jnp.sqrt(gsum(kkv * kkv)), 1e-12)
        k_mod = k * (1.0 + (a - 1.0) * ka_ref[...])
        bonus_s[rows, :] = gsum(r * k_mod * rk_ref[...]) * v

        ltri = ltri_ref[...]
        h1 = _mx(ld)
        h2 = _mx(ld - h1.astype(F32))
        cum = _dot(ltri, h1) + _dot(ltri, h2)
        w_inc = jnp.exp(cum)
        w_inv = jnp.exp(-cum)
        w_exc = jnp.exp(cum - ld)
        at_s[rows, :] = (-kkn * w_exc).astype(at_s.dtype)
        rt_s[rows, :] = (r * w_inc).astype(rt_s.dtype)
        bt_s[rows, :] = (kkn * a * w_inv).astype(bt_s.dtype)
        kt_s[rows, :] = (k_mod * w_inv).astype(kt_s.dtype)
        v_s[rows, :] = v
        wc_s[rows, :] = w_inc

    for b in range(B):
        prepare(b)

    rr = lax.broadcasted_iota(jnp.int32, (SLAB, SLAB), 0) // HEAD
    cc = lax.broadcasted_iota(jnp.int32, (SLAB, SLAB), 1) // HEAD
    diag_f = jnp.where(rr == cc, 1.0, 0.0)
    diag = _mx(diag_f)

    def bdiag(t):
        return jnp.concatenate([t] * (SLAB // HEAD), axis=0) * diag

    def bdiag_t(t):
        tt = jnp.concatenate([t, t], axis=0).T
        return _mx(jnp.concatenate([tt, tt], axis=1)) * diag

    ii = lax.broadcasted_iota(jnp.int32, (CHUNK, SLAB), 0)
    jj = lax.broadcasted_iota(jnp.int32, (CHUNK, SLAB), 1) % HEAD
    strict = ii > jj
    incl = ii >= jj
    eye = (ii == jj).astype(F32)
    chains = [(b, s) for b in range(B) for s in range(R_WIDTH // SLAB)]

    cs = range(len(chains))
    sl = [slice(s * SLAB, (s + 1) * SLAB) for _, s in chains]

    def state_free(c, out):
        rows = [slice(b * T + c * CHUNK, b * T + (c + 1) * CHUNK) for b, _ in chains]
        a_ = [at_s[rows[i], sl[i]] for i in cs]
        r_ = [rt_s[rows[i], sl[i]] for i in cs]
        b_ = [bt_s[rows[i], sl[i]] for i in cs]
        k_ = [kt_s[rows[i], sl[i]] for i in cs]
        vf = [v_s[rows[i], sl[i]] for i in cs]
        ar = [jnp.concatenate([a_[i], r_[i]], axis=0) for i in cs]
        out.update(rows=rows, vf=vf, ar=ar, bk=[jnp.concatenate([b_[i], k_[i]], axis=0) for i in cs])
        mb = [_dot(ar[i], bdiag_t(b_[i].astype(F32))) for i in cs]
        mk = [_dot(ar[i], bdiag_t(k_[i].astype(F32))) for i in cs]
        yield
        a_ab = [jnp.where(strict, mb[i][:CHUNK], 0.0) for i in cs]
        out["a_rb"] = [_mx(jnp.where(incl, mb[i][CHUNK:], 0.0)) for i in cs]
        a_k = [_mx(jnp.concatenate([jnp.where(strict, mk[i][:CHUNK], 0.0),
                                    jnp.where(incl, mk[i][CHUNK:], 0.0)], axis=0)) for i in cs]
        out["kv"] = [_dot(a_k[i], bdiag(_mx(vf[i]))) for i in cs]
        ab = [_mx(t) for t in a_ab]
        xp = [_dot(ab[i], bdiag(ab[i])) for i in cs]
        qs = [eye + a_ab[i] for i in cs]
        n = 2
        while n < CHUNK:
            yield
            xb = [bdiag(_mx(t)) for t in xp]
            if 2 * n < CHUNK:
                res = [_dot(_mx(jnp.concatenate([xp[i], qs[i]], axis=0)), xb[i]) for i in cs]
                xp = [res[i][:CHUNK] for i in cs]
                qs = [qs[i] + res[i][CHUNK:] for i in cs]
            else:
                upd = [_dot(_mx(qs[i]), xb[i]) for i in cs]
                qs = [qs[i] + upd[i] for i in cs]
            n *= 2
        out["inv"] = [_mx(t) for t in qs]

    def state_step(c, pre, state):
        ar, vf, rows = pre["ar"], pre["vf"], pre["rows"]
        asrs = [_dot(ar[i], bdiag_t(state[i])) for i in cs]
        yield
        rhs = [_mx(asrs[i][:CHUNK] + pre["kv"][i][:CHUNK]) for i in cs]
        u = [_dot(pre["inv"][i], bdiag(rhs[i])) for i in cs]
        yield
        yb = [_dot(pre["a_rb"][i], bdiag(_mx(u[i]))) for i in cs]
        uv_t = [_mx(jnp.concatenate([u[i], vf[i]], axis=0).T) for i in cs]
        full = [_dot(uv_t[i], pre["bk"][i]) * diag_f for i in cs]
        for i in cs:
            ds = full[i][0:HEAD]
            for h in range(1, SLAB // HEAD):
                ds = ds + full[i][h * HEAD:(h + 1) * HEAD]
            wc = wc_s[rows[i].stop - 1:rows[i].stop, sl[i]]
            state[i] = (state[i] + ds) * wc
            y_s[rows[i], sl[i]] = asrs[i][CHUNK:] + yb[i] + pre["kv"][i][CHUNK:]

    n_chunks = T // CHUNK
    state = [state_s[i] for i in cs]
    pre = {}
    for _ in state_free(0, pre):
        pass
    for c in range(n_chunks):
        nxt = {}
        streams = [state_step(c, pre, state)]
        if c + 1 < n_chunks:
            streams.append(state_free(c + 1, nxt))
        while streams:
            streams = [g for g in streams if next(g, streams) is not streams]
        pre = nxt
    for i in cs:
        state_s[i] = state[i]

    for b in range(B):
        rows = slice(b * T, (b + 1) * T)
        y = y_s[rows, :]
        mean = gsum(y) * (1.0 / HEAD)
        d = y - mean
        var = gsum(d * d) * (1.0 / HEAD)
        yn = d * lax.rsqrt(var + LNX_EPS) * lw_ref[...] + lb_ref[...]
        y_ref[b] = ((yn + bonus_s[rows, :]) * g_s[rows, :]).astype(y_ref.dtype)


def _rwkv(rw, mu, w0, w2p, a0, a2p, g2p, k_k, k_a, r_k, lnx_w, lnx_b, gmat, ltri, B, S, T):
    const = lambda i: (0, 0)
    vec = pl.BlockSpec((1, R_WIDTH), const)
    big = lambda dt: pltpu.VMEM((B * T, R_WIDTH), dt)
    y = pl.pallas_call(
        _rwkv_kernel,
        grid=(S // T,),
        in_specs=[
            pl.BlockSpec((B, T, R_COLS_PAD), lambda i: (0, i, 0)),
            pl.BlockSpec((1, R_COLS_PAD), const),
            vec,
            pl.BlockSpec((LANES, R_WIDTH), const),
            vec,
            pl.BlockSpec((LANES, R_WIDTH), const),
            pl.BlockSpec((GLAT_PAD, R_WIDTH), const),
            vec, vec, vec, vec, vec,
            pl.BlockSpec((R_WIDTH, R_WIDTH), const),
            pl.BlockSpec((T, T), const),
        ],
        out_specs=pl.BlockSpec((B, T, R_WIDTH), lambda i: (0, i, 0)),
        out_shape=jax.ShapeDtypeStruct((B, S, R_WIDTH), MXU_DTYPE),
        scratch_shapes=[
            pltpu.VMEM((B * (R_WIDTH // SLAB), HEAD, SLAB), F32),
            pltpu.VMEM((B, 8, R_COLS_PAD), F32),
            big(MXU_DTYPE), big(MXU_DTYPE), big(MXU_DTYPE), big(MXU_DTYPE),
            big(F32), big(F32), big(F32), big(F32), big(F32),
        ],
        compiler_params=pltpu.CompilerParams(
            dimension_semantics=("arbitrary",), vmem_limit_bytes=VMEM_LIMIT["rwkv"]),
        name="rwkv",
    )(rw.reshape(B, S, R_COLS_PAD), mu, w0, w2p, a0, a2p, g2p, k_k, k_a, r_k, lnx_w, lnx_b, gmat, ltri)
    return y.reshape(B * S, R_WIDTH)


def _cmp_kernel(x_ref, w1_ref, b1_ref, w2_ref, pe_ref, pos_ref, freq_ref, sgn_ref, o_ref, ot_ref):
    n_blk = x_ref.shape[0]
    half = x_ref.shape[1]
    j = pl.program_id(1)
    xb = _mx(x_ref[...])
    w1 = w1_ref[...]
    first_half = _dot(xb, w1[:half])
    second_half = pltpu.roll(_dot(xb, w1[half:]), n_blk - 1, 0)
    bias = _dot(_mx(pe_ref[...]), w1)[0:1] + b1_ref[...]
    hdn = first_half + second_half + bias
    hdn = hdn * jax.nn.sigmoid(hdn)
    out = _dot(_mx(hdn), w2_ref[...])
    cos, sin = _rope_tables(pos_ref[...], freq_ref[...], sgn_ref[...])
    lane = lax.broadcasted_iota(jnp.int32, (1, LANES), 1)
    roped = _rope128(out, cos, sin, (lane % HEAD) < ROT_HALF)
    res = jnp.where(j < N_KV_HEADS, roped, out)
    row = lax.broadcasted_iota(jnp.int32, (n_blk, 1), 0)
    res = jnp.where(row == n_blk - 1, 0.0, res)
    o_ref[...] = res[:, :HEAD].astype(o_ref.dtype)
    ot_ref[...] = res.T[:HEAD].astype(ot_ref.dtype)


def _compress(xblk, w1s, b1s, w2s, pes, pos_c, freq, sgn, B, n_blk):
    half = xblk.shape[-1]
    return pl.pallas_call(
        _cmp_kernel,
        grid=(B, 2 * N_KV_HEADS),
        in_specs=[
            pl.BlockSpec((None, None, n_blk, half), lambda b, j: (j, b, 0, 0)),
            pl.BlockSpec((None, 2 * half, CMP_HIDDEN), lambda b, j: (j // N_KV_HEADS, 0, 0)),
            pl.BlockSpec((None, 1, CMP_HIDDEN), lambda b, j: (j // N_KV_HEADS, 0, 0)),
            pl.BlockSpec((None, CMP_HIDDEN, LANES), lambda b, j: (j // N_KV_HEADS, 0, 0)),
            pl.BlockSpec((None, 8, 2 * half), lambda b, j: (j // N_KV_HEADS, 0, 0)),
            pl.BlockSpec((None, n_blk, 1), lambda b, j: (b, 0, 0)),
            pl.BlockSpec((1, LANES), lambda b, j: (0, 0)),
            pl.BlockSpec((1, LANES), lambda b, j: (0, 0)),
        ],
        out_specs=[
            pl.BlockSpec((None, None, n_blk, HEAD), lambda b, j: (b, j, 0, 0)),
            pl.BlockSpec((None, None, HEAD, n_blk), lambda b, j: (b, j, 0, 0)),
        ],
        out_shape=[
            jax.ShapeDtypeStruct((B, 2 * N_KV_HEADS, n_blk, HEAD), MXU_DTYPE),
            jax.ShapeDtypeStruct((B, 2 * N_KV_HEADS, HEAD, n_blk), MXU_DTYPE),
        ],
        compiler_params=pltpu.CompilerParams(
            dimension_semantics=("parallel", "parallel"), vmem_limit_bytes=VMEM_LIMIT["compress"]),
        name="compress",
    )(xblk, w1s, b1s, w2s, pes, pos_c, freq, sgn)


def _nsa_kernel(q_ref, kc_ref, vct_ref, ks_ref, vst_ref, kw_ref, vwt_ref, g_ref, o_ref,
                sel_s, ocmp_s, owin_s, sbuf_s, *, n_top, levels):
    n_blk = kc_ref.shape[0]
    n_sel = sel_s.shape[0]
    hk = pl.program_id(1)
    qb = pl.program_id(2)
    start = qb * Q_BLOCK
    W = GQA * Q_BLOCK

    q_t = q_ref[...].astype(F32).T
    qt = _mx(jnp.concatenate([q_t[g * HEAD:(g + 1) * HEAD] for g in range(GQA)], axis=1))
    tile4 = lambda t: jnp.concatenate([t] * GQA, axis=1)
    qi = lax.broadcasted_iota(jnp.int32, (1, Q_BLOCK), 1)
    t_q = start + qi
    t_w = tile4(t_q)

    tiles = GROUP // KV_TILE

    def compressed_and_select(n_c, n_s):
        w0 = pl.multiple_of(jnp.maximum(start - WINDOW, 0), LANES)
        n_win = WINDOW + Q_BLOCK
        s_win = _dot(kw_ref[pl.ds(w0, n_win), :], qt)

        c_end = lax.broadcasted_iota(jnp.int32, (n_c, 1), 0) * CMP_STRIDE + (CMP_BLOCK - 1)
        sc = _dot(kc_ref[0:n_c, :], qt) + tile4(jnp.where(c_end <= t_q, 0.0, MASK))
        q_plain = jnp.concatenate([qt, jnp.zeros((LANES - HEAD, W), MXU_DTYPE)], axis=0)
        s_first = [_dot(ks_ref[i * KV_TILE:(i + 1) * KV_TILE, :], q_plain) for i in range(tiles)]

        tok = w0 + lax.broadcasted_iota(jnp.int32, (n_win, 1), 0)
        in_win = (tok <= t_q) & (tok > t_q - WINDOW)
        s_win = s_win + tile4(jnp.where(in_win, 0.0, MASK))
        p_win = jnp.exp2(s_win - jnp.max(s_win, axis=0, keepdims=True))
        l_win = jnp.sum(p_win, axis=0, keepdims=True)
        p_win = _mx(p_win)
        acc_win = jnp.zeros((HEAD, W), F32)
        for i in range(n_win // LANES):
            acc_win = acc_win + _dot(vwt_ref[w0 // LANES + i], p_win[i * LANES:(i + 1) * LANES])
        owin_s[...] = acc_win * (1.0 / l_win)

        m = jnp.max(sc, axis=0, keepdims=True)
        p = jnp.exp2(sc - m)
        l = jnp.sum(p, axis=0, keepdims=True)
        pn = p * jnp.where(t_w >= CMP_BLOCK - 1, 1.0 / l, 0.0)
        ocmp_s[...] = _dot(vct_ref[:, 0:n_c], _mx(pn))

        ps = pn[:, 0:Q_BLOCK]
        for g in range(1, GQA):
            ps = ps + pn[:, g * Q_BLOCK:(g + 1) * Q_BLOCK]
        hi = _mx(ps)
        lo = _mx(ps - hi.astype(F32))
        j_col = lax.broadcasted_iota(jnp.int32, (n_s, 1), 0)
        c_row = lax.broadcasted_iota(jnp.int32, (1, n_c), 1)
        cover = ((c_row * CMP_STRIDE < j_col * SEL_BLOCK + SEL_BLOCK)
                 & (c_row * CMP_STRIDE + CMP_BLOCK > j_col * SEL_BLOCK))
        cover = jnp.where(cover, 1.0, 0.0).astype(MXU_DTYPE)
        imp = _dot(cover, hi) + _dot(cover, lo)

        cur = t_q // SEL_BLOCK
        forced = (j_col == 0) | (j_col == cur) | (j_col == cur - 1)
        score = jnp.where(forced, REMOVED, jnp.where(j_col <= cur, imp, MASK))
        for _ in range(max(n_top - 3, 0)):
            mx = jnp.max(score, axis=0, keepdims=True)
            idx = jnp.min(jnp.where(score == mx, j_col, n_s), axis=0, keepdims=True)
            score = jnp.where(j_col == idx, REMOVED, score)
        sel = jnp.where(score == REMOVED, 1.0, 0.0)
        sel_s[0:n_s, :] = sel

        neg = jnp.where(sel[0:SEL_GROUP] > 0.5, 0.0, MASK)
        per_tile = KV_TILE // SEL_BLOCK
        for i in range(tiles):
            bias = jnp.concatenate([jnp.broadcast_to(neg[r:r + 1], (SEL_BLOCK, Q_BLOCK))
                                    for r in range(i * per_tile, (i + 1) * per_tile)], axis=0)
            sbuf_s[0, i * KV_TILE:(i + 1) * KV_TILE, :] = s_first[i] + tile4(bias)

    lo_qb = 0
    for hi_qb in levels:
        n_c = min(hi_qb * (Q_BLOCK // CMP_STRIDE), n_blk)
        n_s = min(hi_qb * (Q_BLOCK // SEL_BLOCK), n_sel)
        pl.when((qb >= lo_qb) & (qb < hi_qb))(functools.partial(compressed_and_select, n_c, n_s))
        lo_qb = hi_qb
    o_cmp = ocmp_s[...]

    zero_rows = jnp.zeros((LANES - HEAD - 2 * SEL_GROUP, W), MXU_DTYPE)

    def group_scores(j):
        grp = sel_s[pl.ds(pl.multiple_of(j * SEL_GROUP, SEL_GROUP), SEL_GROUP), :]
        neg = jnp.where(grp > 0.5, 0.0, MASK)
        neg = tile4(jnp.concatenate([neg, jnp.zeros_like(neg)], axis=0))
        q_aug = jnp.concatenate([qt, _mx(neg), zero_rows], axis=0)
        r0 = pl.multiple_of(j * GROUP, GROUP)
        return [_dot(ks_ref[pl.ds(r0 + i * KV_TILE, KV_TILE), :], q_aug) for i in range(tiles)]

    def put_scores(slot, j):
        for i, s in enumerate(group_scores(j)):
            sbuf_s[slot, i * KV_TILE:(i + 1) * KV_TILE, :] = s

    def online(s, vt_tile, carry):
        m_, acc = carry
        m_new = jnp.maximum(m_, jnp.max(s, axis=0, keepdims=True))
        alpha = jnp.exp2(m_ - m_new)
        return m_new, alpha * acc + _dot(vt_tile, _mx(jnp.exp2(s - m_new)))

    def consume(slot, j, carry):
        for i in range(tiles):
            s = sbuf_s[slot, i * KV_TILE:(i + 1) * KV_TILE, :]
            carry = online(s, vst_ref[j * tiles + i], carry)
        return carry

    def pair_body(jp, carry):
        j = 2 * jp
        put_scores(1, j + 1)
        carry = consume(0, j, carry)
        put_scores(0, j + 2)
        return consume(1, j + 1, carry)

    def single_body(j, carry):
        carry = consume(0, j, carry)
        put_scores(0, j + 1)
        return carry

    init = (jnp.full((1, W), MASK, F32), jnp.zeros((HEAD + ONES_ROWS, W), F32))
    last = qb // (GROUP // Q_BLOCK)
    quad_body = lambda jq, c: pair_body(2 * jq + 1, pair_body(2 * jq, c))
    oct_body = lambda jo, c: quad_body(2 * jo + 1, quad_body(2 * jo, c))
    carry = lax.fori_loop(0, last // 8, oct_body, init)
    carry = lax.fori_loop(2 * (last // 8), last // 4, quad_body, carry)
    carry = lax.fori_loop(2 * (last // 4), last // 2, pair_body, carry)
    carry = lax.fori_loop(2 * (last // 2), last, single_body, carry)

    k_row = lax.broadcasted_iota(jnp.int32, (KV_TILE, 1), 0)
    for i in range(tiles):
        tok = last * GROUP + i * KV_TILE + k_row
        causal = tile4(jnp.where(tok <= t_q, 0.0, MASK))
        s = sbuf_s[0, i * KV_TILE:(i + 1) * KV_TILE, :] + causal
        carry = online(s, vst_ref[last * tiles + i], carry)
    _, acc_sel = carry
    o_sel = acc_sel[:HEAD] * (1.0 / acc_sel[HEAD:HEAD + 1])
    o_win = owin_s[...]

    outs = []
    gates = g_ref[...]
    per_kv = GQA * 3
    gate = lambda r: jnp.where(hk == 0, gates[r:r + 1], gates[per_kv + r:per_kv + r + 1])
    for g in range(GQA):
        cols = slice(g * Q_BLOCK, (g + 1) * Q_BLOCK)
        o_g = (gate(g * 3) * o_cmp[:, cols] + gate(g * 3 + 1) * o_sel[:, cols]
               + gate(g * 3 + 2) * o_win[:, cols])
        outs.append(o_g)
    o_ref[...] = jnp.concatenate(outs, axis=0).T.astype(o_ref.dtype)


def _nsa(q, kc, vct, ks, vst, kw, vwt, gt, B, S):
    BS = q.shape[0]
    nqb = S // Q_BLOCK
    n_blk = S // CMP_STRIDE
    n_sel = S // SEL_BLOCK
    n_top = min(SEL_TOPK, n_sel)
    step = max(4, nqb // CMP_VARIANTS // 4 * 4)
    levels = tuple(range(step, nqb, step)) + (nqb,)
    per_head = lambda *blk: pl.BlockSpec((None, None) + blk, lambda b, h, i: (b, h) + (0,) * len(blk))
    qspec = pl.BlockSpec((Q_BLOCK, GQA * HEAD), lambda b, h, i: (b * nqb + i, h))
    return pl.pallas_call(
        functools.partial(_nsa_kernel, n_top=n_top, levels=levels),
        grid=(B, N_KV_HEADS, nqb),
        in_specs=[
            qspec,
            per_head(n_blk, HEAD),
            per_head(HEAD, n_blk),
            per_head(S, LANES),
            per_head(S // KV_TILE, HEAD + ONES_ROWS, KV_TILE),
            per_head(S, HEAD),
            per_head(S // LANES, HEAD, LANES),
            pl.BlockSpec((None, GATE_ROWS, Q_BLOCK), lambda b, h, i: (b, 0, i)),
        ],
        out_specs=qspec,
        out_shape=jax.ShapeDtypeStruct((BS, N_WIDTH), MXU_DTYPE),
        scratch_shapes=[
            pltpu.VMEM((n_sel, Q_BLOCK), F32),
            pltpu.VMEM((HEAD, GQA * Q_BLOCK), F32),
            pltpu.VMEM((HEAD, GQA * Q_BLOCK), F32),
            pltpu.VMEM((2, GROUP, GQA * Q_BLOCK), F32),
        ],
        compiler_params=pltpu.CompilerParams(
            dimension_semantics=("parallel", "parallel", "arbitrary"), vmem_limit_bytes=VMEM_LIMIT["nsa"]),
        name="nsa",
    )(q, kc, vct, ks, vst, kw, vwt, gt)


def _ffn_kernel(x_ref, yr_ref, yn_ref, wo_ref, nf_ref, wg_ref, wu_ref, wd_ref, nfin_ref, o_ref,
                h1_s, xn_s, acc_s):
    f = pl.program_id(1)

    @pl.when(f == 0)
    def _():
        mix = _dot(yr_ref[...], wo_ref[:R_WIDTH]) + _dot(yn_ref[...], wo_ref[R_WIDTH:])
        h1 = x_ref[...] + mix
        h1_s[...] = h1
        xn_s[...] = _rms(h1, nf_ref[...]).astype(xn_s.dtype)
        acc_s[...] = jnp.zeros_like(acc_s)

    xn = xn_s[...]
    gate = _dot(xn, wg_ref[...])
    up = _dot(xn, wu_ref[...])
    acc_s[...] += _dot(_mx(gate * jax.nn.sigmoid(gate) * up), wd_ref[...])

    @pl.when(f == pl.num_programs(1) - 1)
    def _():
        o_ref[...] = _rms(h1_s[...] + acc_s[...], nfin_ref[...])


def _ffn(x2, y_r, y_n, w_out, nf, w_gate, w_up, w_down, nfin, tm, tf):
    BS, D = x2.shape
    d_ff = w_gate.shape[1]
    row = lambda i, f: (i, 0)
    const = lambda i, f: (0, 0)
    return pl.pallas_call(
        _ffn_kernel,
        grid=(BS // tm, d_ff // tf),
        in_specs=[
            pl.BlockSpec((tm, D), row),
            pl.BlockSpec((tm, R_WIDTH), row),
            pl.BlockSpec((tm, N_WIDTH), row),
            pl.BlockSpec((R_WIDTH + N_WIDTH, D), const),
            pl.BlockSpec((1, D), const),
            pl.BlockSpec((D, tf), lambda i, f: (0, f)),
            pl.BlockSpec((D, tf), lambda i, f: (0, f)),
            pl.BlockSpec((tf, D), lambda i, f: (f, 0)),
            pl.BlockSpec((1, D), const),
        ],
        out_specs=pl.BlockSpec((tm, D), row),
        out_shape=jax.ShapeDtypeStruct((BS, D), F32),
        scratch_shapes=[pltpu.VMEM((tm, D), F32), pltpu.VMEM((tm, D), MXU_DTYPE), pltpu.VMEM((tm, D), F32)],
        compiler_params=pltpu.CompilerParams(
            dimension_semantics=("parallel", "arbitrary"), vmem_limit_bytes=VMEM_LIMIT["ffn"]),
        name="ffn",
    )(x2, y_r, y_n, w_out, nf, w_gate, w_up, w_down, nfin)


def _pad_to(a, n, axis):
    pad = [(0, 0)] * a.ndim
    pad[axis] = (0, n - a.shape[axis])
    return jnp.pad(a, pad)


def _rope_rows():
    lane = jnp.arange(LANES) % HEAD
    inv_freq = jnp.power(jnp.float32(ROPE_THETA), -jnp.arange(ROT_HALF, dtype=F32) * 2.0 / ROT_DIM)
    freq = jnp.where(lane < ROT_DIM, inv_freq[lane % ROT_HALF], 0.0).astype(F32)[None, :]
    sgn = jnp.where(lane < ROT_HALF, -1.0, jnp.where(lane < ROT_DIM, 1.0, 0.0)).astype(F32)[None, :]
    return freq, sgn


def _layer(h, positions, norm_mix, w_in, mu_shift, w0, w2, a0, a2, g2, k_k, k_a, r_k, lnx_w, lnx_b,
           pe_k, wk1, bk1, wk2, pe_v, wv1, bv1, wv2, w_out, norm_ffn, w_gate, w_up, w_down, norm_final):
    B, S, D = h.shape
    BS = B * S
    x2 = h.reshape(BS, D)
    freq, sgn = _rope_rows()
    row = lambda a: a.reshape(1, -1).astype(F32)

    w_r = _mx(_pad_to(w_in[:, :R_COLS], R_COLS_PAD, 1))
    w_n = _mx(_pad_to(w_in[:, R_COLS:], N_COLS_PAD, 1))
    tiles = _tiles(S, BS, w_gate.shape[1])
    rw, q, kvc, ks, vst, kw, vwt, gt = _proj(
        x2, positions.reshape(BS, 1), row(norm_mix), freq, sgn, w_r, w_n, B, S, tiles["proj_rows"])

    T = tiles["rwkv_rows"]
    head_of = jnp.arange(R_WIDTH) // HEAD
    gmat = (head_of[:, None] == head_of[None, :]).astype(MXU_DTYPE)
    tt = jnp.arange(T)
    ltri = ((tt[:, None] // CHUNK == tt[None, :] // CHUNK) & (tt[:, None] >= tt[None, :])).astype(MXU_DTYPE)
    w2p = _mx(_pad_to(w2, LANES, 0))
    a2p = _mx(jnp.concatenate([jnp.zeros((DECAY_LORA, R_WIDTH), F32), a2], axis=0))
    g2p = _mx(_pad_to(g2, GLAT_PAD, 0))
    y_r = _rwkv(rw, row(_pad_to(mu_shift, R_COLS_PAD, 0)), row(w0), w2p, row(a0), a2p, g2p,
                row(k_k), row(k_a), row(r_k), row(lnx_w), row(lnx_b), gmat, ltri, B, S, T)

    n_blk = S // CMP_STRIDE
    xblk = kvc.reshape(2 * N_KV_HEADS, B, n_blk, CMP_STRIDE * HEAD)
    w1s = _mx(jnp.stack([wk1, wv1]))
    b1s = jnp.stack([bk1, bv1]).reshape(2, 1, CMP_HIDDEN).astype(F32)
    w2s = _mx(_pad_to(jnp.stack([wk2, wv2]), LANES, 2))
    pes = _pad_to(jnp.stack([pe_k, pe_v]).reshape(2, 1, CMP_BLOCK * HEAD), 8, 1).astype(F32)
    pos_c = _pad_to(positions[:, CMP_BLOCK - 1::CMP_STRIDE], n_blk, 1).reshape(B, n_blk, 1)
    cmp_rows, cmp_t = _compress(xblk, w1s, b1s, w2s, pes, pos_c, freq, sgn, B, n_blk)
    kc = cmp_rows[:, :N_KV_HEADS]
    vct = cmp_t[:, N_KV_HEADS:]

    y_n = _nsa(q, kc, vct, ks, vst, kw, vwt, gt, B, S)

    out = _ffn(x2, y_r, y_n, _mx(w_out), row(norm_ffn), _mx(w_gate), _mx(w_up), _mx(w_down),
               row(norm_final), tiles["ffn_rows"], tiles["ffn_cols"])
    return out.reshape(B, S, D)


def kernel(x, positions, norm_mix, w_in, mu_shift, w0, w2, a0, a2, g2, k_k, k_a, r_k, lnx_w, lnx_b, pe_k, wk1, bk1, wk2, pe_v, wv1, bv1, wv2, w_out, norm_ffn, w_gate, w_up, w_down, norm_final):
    assert norm_mix.shape[0] == 1, "single-layer block"
    l = 0
    return _layer(x, positions, norm_mix[l], w_in[l], mu_shift[l], w0[l], w2[l], a0[l], a2[l], g2[l],
                  k_k[l], k_a[l], r_k[l].reshape(-1), lnx_w[l], lnx_b[l], pe_k[l], wk1[l], bk1[l], wk2[l],
                  pe_v[l], wv1[l], bv1[l], wv2[l], w_out[l], norm_ffn[l], w_gate[l], w_up[l], w_down[l],
                  norm_final)
```

```python
import functools

import jax
import jax.numpy as jnp
from jax import lax
from jax.experimental import pallas as pl
from jax.experimental.pallas import tpu as pltpu

F32 = jnp.float32
MXU_DTYPE = jnp.bfloat16

R_HEADS = 8
HEAD = 64
R_WIDTH = R_HEADS * HEAD
DECAY_LORA = 64
ICLR_LORA = 64
GATE_LORA = 160
LNX_EPS = 64e-5
R_COLS = 3 * R_WIDTH + DECAY_LORA + ICLR_LORA + GATE_LORA
N_Q_HEADS = 8
N_KV_HEADS = 2
GQA = N_Q_HEADS // N_KV_HEADS
N_WIDTH = N_Q_HEADS * HEAD
KV_WIDTH = N_KV_HEADS * HEAD
CMP_BLOCK = 32
CMP_STRIDE = 16
CMP_HIDDEN = 256
SEL_BLOCK = 64
SEL_TOPK = 16
WINDOW = 512
Q_BLOCK = 128
ROPE_THETA = 500000.0
ROT_DIM = HEAD // 4
ROT_HALF = ROT_DIM // 2
NORM_EPS = 1e-6
MASK = -1e30
FORCE = 1e6
REMOVED = -(2.0 ** 127)

LANES = 128
R_COLS_PAD = 1920
GLAT_PAD = 256
N_COLS_PAD = N_WIDTH + 6 * KV_WIDTH + LANES
GATE_ROWS = 32
CHUNK = 64
SLAB = 4 * HEAD
SEL_GROUP = 8
GROUP = SEL_GROUP * SEL_BLOCK
KV_TILE = 256
ONES_ROWS = 16
LOG2E = 1.4426950408889634
EXP_M05 = 0.6065306597126334
CMP_VARIANTS = 8
PROJ_SUB = 256
MIB = 1024 * 1024
VMEM_LIMIT = {"proj": 32 * MIB, "rwkv": 28 * MIB, "compress": 24 * MIB, "nsa": 44 * MIB, "ffn": 48 * MIB}


def _tiles(S, BS, d_ff):
    half_ff = d_ff // 2
    return dict(proj_rows=min(512, S), rwkv_rows=min(256, S), ffn_rows=min(512, BS),
                ffn_cols=half_ff if half_ff % LANES == 0 else d_ff)


def _dot(a, b):
    return jnp.dot(a, b, preferred_element_type=F32)


def _dot_nt(a, b):
    return lax.dot_general(a, b, (((1,), (1,)), ((), ())), preferred_element_type=F32)


def _mx(a):
    return a.astype(MXU_DTYPE)


def _rms(x, w):
    ms = jnp.mean(x * x, axis=-1, keepdims=True)
    return x * lax.rsqrt(ms + NORM_EPS) * w


def _rope_tables(pos_col, freq_row, sgn_row):
    ang = pos_col.astype(F32) * freq_row
    return jnp.cos(ang), jnp.sin(ang) * sgn_row


def _rope128(t, cos, sin, first):
    sh = jnp.where(first, pltpu.roll(t, LANES - ROT_HALF, 1), pltpu.roll(t, ROT_HALF, 1))
    return t * cos + sh * sin


def _proj_kernel(x_ref, pos_ref, nw_ref, freq_ref, sgn_ref, wr_ref, wn_ref,
                 rw_ref, q_ref, kvc_ref, ks_ref, vst_ref, kw_ref, vwt_ref, gt_ref):
    tm = x_ref.shape[0]
    sub = min(tm, PROJ_SUB)
    lane = lax.broadcasted_iota(jnp.int32, (1, LANES), 1)
    first = (lane % HEAD) < ROT_HALF
    ones_rows = jnp.where(lax.broadcasted_iota(jnp.int32, (ONES_ROWS, KV_TILE), 0) == 0, 1.0, 0.0)
    scale = HEAD ** -0.5 * LOG2E

    for r0 in range(0, tm, sub):
        rows = slice(r0, r0 + sub)
        xn = _mx(_rms(x_ref[rows, :], nw_ref[...]))
        rw_ref[rows, :] = _dot(xn, wr_ref[...])
        pn = _dot(xn, wn_ref[...])
        cos, sin = _rope_tables(pos_ref[rows, :], freq_ref[...], sgn_ref[...])
        rope = lambda t: _rope128(t, cos, sin, first)
        q = [rope(pn[:, i * LANES:(i + 1) * LANES]) * scale for i in range(N_WIDTH // LANES)]
        q_ref[rows, :] = jnp.concatenate(q, axis=1).astype(q_ref.dtype)
        o = N_WIDTH
        for j in range(2 * N_KV_HEADS):
            kvc_ref[j, rows, :] = pn[:, o + j * HEAD:o + (j + 1) * HEAD].astype(kvc_ref.dtype)
        o += 2 * KV_WIDTH

        row_g = pl.program_id(1) * tm + r0 + lax.broadcasted_iota(jnp.int32, (sub, 1), 0)
        onehot = jnp.where(lane == HEAD + (row_g // SEL_BLOCK) % SEL_GROUP, 1.0, 0.0)
        kk = rope(pn[:, o:o + KV_WIDTH])
        vt = rope(pn[:, o + KV_WIDTH:o + 2 * KV_WIDTH]).T
        o += 2 * KV_WIDTH
        for h in range(N_KV_HEADS):
            k_h = kk if h == 0 else pltpu.roll(kk, HEAD, 1)
            ks_ref[h, rows, :] = jnp.where(lane < HEAD, k_h, onehot).astype(ks_ref.dtype)
            for i in range(sub // KV_TILE):
                v_t = vt[h * HEAD:(h + 1) * HEAD, i * KV_TILE:(i + 1) * KV_TILE]
                vst_ref[h, r0 // KV_TILE + i] = jnp.concatenate([v_t, ones_rows], axis=0).astype(vst_ref.dtype)

        kk = rope(pn[:, o:o + KV_WIDTH])
        vt = rope(pn[:, o + KV_WIDTH:o + 2 * KV_WIDTH]).T
        o += 2 * KV_WIDTH
        for h in range(N_KV_HEADS):
            kw_ref[h, rows, :] = kk[:, h * HEAD:(h + 1) * HEAD].astype(kw_ref.dtype)
            for i in range(sub // LANES):
                vwt_ref[h, r0 // LANES + i] = vt[h * HEAD:(h + 1) * HEAD,
                                                 i * LANES:(i + 1) * LANES].astype(vwt_ref.dtype)
        gt_ref[:, rows] = jax.nn.sigmoid(pn[:, o:o + LANES]).T[:GATE_ROWS]


def _proj(x2, pos_col, nw, freq, sgn, w_r, w_n, B, S, tm):
    BS, D = x2.shape
    nt = S // tm
    nkt = S // LANES
    row = lambda b, i: (b * nt + i, 0)
    const = lambda b, i: (0, 0)
    head_rows = lambda width: pl.BlockSpec((None, N_KV_HEADS, tm, width), lambda b, i: (b, 0, i, 0))
    head_tiles = lambda rows, width: pl.BlockSpec(
        (None, N_KV_HEADS, tm // width, rows, width), lambda b, i: (b, 0, i, 0, 0))
    kv = lambda *shape: jax.ShapeDtypeStruct((B, N_KV_HEADS) + shape, MXU_DTYPE)
    return pl.pallas_call(
        _proj_kernel,
        grid=(B, nt),
        in_specs=[
            pl.BlockSpec((tm, D), row),
            pl.BlockSpec((tm, 1), row),
            pl.BlockSpec((1, D), const),
            pl.BlockSpec((1, LANES), const),
            pl.BlockSpec((1, LANES), const),
            pl.BlockSpec((D, R_COLS_PAD), const),
            pl.BlockSpec((D, N_COLS_PAD), const),
        ],
        out_specs=[
            pl.BlockSpec((tm, R_COLS_PAD), row),
            pl.BlockSpec((tm, N_WIDTH), row),
            pl.BlockSpec((2 * N_KV_HEADS, tm, HEAD), lambda b, i: (0, b * nt + i, 0)),
            head_rows(LANES), head_tiles(HEAD + ONES_ROWS, KV_TILE),
            head_rows(HEAD), head_tiles(HEAD, LANES),
            pl.BlockSpec((None, GATE_ROWS, tm), lambda b, i: (b, 0, i)),
        ],
        out_shape=[
            jax.ShapeDtypeStruct((BS, R_COLS_PAD), F32),
            jax.ShapeDtypeStruct((BS, N_WIDTH), MXU_DTYPE),
            jax.ShapeDtypeStruct((2 * N_KV_HEADS, BS, HEAD), MXU_DTYPE),
            kv(S, LANES), kv(S // KV_TILE, HEAD + ONES_ROWS, KV_TILE),
            kv(S, HEAD), kv(nkt, HEAD, LANES),
            jax.ShapeDtypeStruct((B, GATE_ROWS, S), F32),
        ],
        compiler_params=pltpu.CompilerParams(
            dimension_semantics=("parallel", "parallel"), vmem_limit_bytes=VMEM_LIMIT["proj"]),
        name="proj",
    )(x2, pos_col, nw, freq, sgn, w_r, w_n)


def _rwkv_kernel(rw_ref, mu_ref, w0_ref, w2_ref, a0_ref, a2_ref, g2_ref, kk_ref, ka_ref, rk_ref,
                 lw_ref, lb_ref, gmat_ref, ltri_ref, y_ref,
                 state_s, prev_s, at_s, rt_s, bt_s, kt_s, v_s, wc_s, y_s, bonus_s, g_s):
    B, T = rw_ref.shape[0], rw_ref.shape[1]

    @pl.when(pl.program_id(0) == 0)
    def _():
        state_s[...] = jnp.zeros_like(state_s)
        prev_s[...] = jnp.zeros_like(prev_s)

    gmat = gmat_ref[...]

    def gsum(t):
        return _dot(_mx(t), gmat)

    def prepare(b):
        rows = slice(b * T, (b + 1) * T)
        x = rw_ref[b]
        row = lax.broadcasted_iota(jnp.int32, (T, 1), 0)
        xs = jnp.where(row == 0, prev_s[b, 0:1, :], pltpu.roll(x, 1, 0))
        prev_s[b, 0:1, :] = x[T - 1:T, :]
        xm = x + (xs - x) * mu_ref[...]
        r = xm[:, 0:R_WIDTH]
        k = xm[:, R_WIDTH:2 * R_WIDTH]
        v = xm[:, 2 * R_WIDTH:3 * R_WIDTH]
        lat = xm[:, 3 * R_WIDTH:3 * R_WIDTH + LANES]
        glat = xm[:, 3 * R_WIDTH + LANES:3 * R_WIDTH + LANES + GLAT_PAD]

        wv = w0_ref[...] + _dot(_mx(jnp.tanh(lat)), w2_ref[...])
        ld = -EXP_M05 * jax.nn.sigmoid(wv)
        a = jax.nn.sigmoid(a0_ref[...] + _dot(_mx(lat), a2_ref[...]))
        g_s[rows, :] = _dot(_mx(jax.nn.sigmoid(glat)), g2_ref[...])
        kkv = k * kk_ref[...]
        kkn = kkv / jnp.maximum(jnp.sqrt(gsum(kkv * kkv)), 1e-12)
        k_mod = k * (1.0 + (a - 1.0) * ka_ref[...])
        bonus_s[rows, :] = gsum(r * k_mod * rk_ref[...]) * v

        ltri = ltri_ref[...]
        h1 = _mx(ld)
        h2 = _mx(ld - h1.astype(F32))
        cum = _dot(ltri, h1) + _dot(ltri, h2)
        w_inc = jnp.exp(cum)
        w_inv = jnp.exp(-cum)
        w_exc = jnp.exp(cum - ld)
        at_s[rows, :] = (-kkn * w_exc).astype(at_s.dtype)
        rt_s[rows, :] = (r * w_inc).astype(rt_s.dtype)
        bt_s[rows, :] = (kkn * a * w_inv).astype(bt_s.dtype)
        kt_s[rows, :] = (k_mod * w_inv).astype(kt_s.dtype)
        v_s[rows, :] = v
        wc_s[rows, :] = w_inc

    for b in range(B):
        prepare(b)

    rr = lax.broadcasted_iota(jnp.int32, (SLAB, SLAB), 0) // HEAD
    cc = lax.broadcasted_iota(jnp.int32, (SLAB, SLAB), 1) // HEAD
    diag_f = jnp.where(rr == cc, 1.0, 0.0)
    diag = _mx(diag_f)

    def bdiag(t):
        return jnp.concatenate([t] * (SLAB // HEAD), axis=0) * diag

    def bdiag_t(t):
        tt = jnp.concatenate([t, t], axis=0).T
        return _mx(jnp.concatenate([tt, tt], axis=1)) * diag

    ii = lax.broadcasted_iota(jnp.int32, (CHUNK, SLAB), 0)
    jj = lax.broadcasted_iota(jnp.int32, (CHUNK, SLAB), 1) % HEAD
    strict = ii > jj
    incl = ii >= jj
    eye = (ii == jj).astype(F32)
    chains = [(b, s) for b in range(B) for s in range(R_WIDTH // SLAB)]

    cs = range(len(chains))
    sl = [slice(s * SLAB, (s + 1) * SLAB) for _, s in chains]

    def state_free(c, out):
        rows = [slice(b * T + c * CHUNK, b * T + (c + 1) * CHUNK) for b, _ in chains]
        a_ = [at_s[rows[i], sl[i]] for i in cs]
        r_ = [rt_s[rows[i], sl[i]] for i in cs]
        b_ = [bt_s[rows[i], sl[i]] for i in cs]
        k_ = [kt_s[rows[i], sl[i]] for i in cs]
        vf = [v_s[rows[i], sl[i]] for i in cs]
        ar = [jnp.concatenate([a_[i], r_[i]], axis=0) for i in cs]
        out.update(rows=rows, vf=vf, ar=ar, bk=[jnp.concatenate([b_[i], k_[i]], axis=0) for i in cs])
        mb = [_dot(ar[i], bdiag_t(b_[i].astype(F32))) for i in cs]
        mk = [_dot(ar[i], bdiag_t(k_[i].astype(F32))) for i in cs]
        yield
        a_ab = [jnp.where(strict, mb[i][:CHUNK], 0.0) for i in cs]
        out["a_rb"] = [_mx(jnp.where(incl, mb[i][CHUNK:], 0.0)) for i in cs]
        a_k = [_mx(jnp.concatenate([jnp.where(strict, mk[i][:CHUNK], 0.0),
                                    jnp.where(incl, mk[i][CHUNK:], 0.0)], axis=0)) for i in cs]
        out["kv"] = [_dot(a_k[i], bdiag(_mx(vf[i]))) for i in cs]
        ab = [_mx(t) for t in a_ab]
        xp = [_dot(ab[i], bdiag(ab[i])) for i in cs]
        qs = [eye + a_ab[i] for i in cs]
        n = 2
        while n < CHUNK:
            yield
            xb = [bdiag(_mx(t)) for t in xp]
            if 2 * n < CHUNK:
                res = [_dot(_mx(jnp.concatenate([xp[i], qs[i]], axis=0)), xb[i]) for i in cs]
                xp = [res[i][:CHUNK] for i in cs]
                qs = [qs[i] + res[i][CHUNK:] for i in cs]
            else:
                upd = [_dot(_mx(qs[i]), xb[i]) for i in cs]
                qs = [qs[i] + upd[i] for i in cs]
            n *= 2
        out["inv"] = [_mx(t) for t in qs]

    def state_step(c, pre, state):
        ar, vf, rows = pre["ar"], pre["vf"], pre["rows"]
        asrs = [_dot(ar[i], bdiag_t(state[i])) for i in cs]
        yield
        rhs = [_mx(asrs[i][:CHUNK] + pre["kv"][i][:CHUNK]) for i in cs]
        u = [_dot(pre["inv"][i], bdiag(rhs[i])) for i in cs]
        yield
        yb = [_dot(pre["a_rb"][i], bdiag(_mx(u[i]))) for i in cs]
        uv_t = [_mx(jnp.concatenate([u[i], vf[i]], axis=0).T) for i in cs]
        full = [_dot(uv_t[i], pre["bk"][i]) * diag_f for i in cs]
        for i in cs:
            ds = full[i][0:HEAD]
            for h in range(1, SLAB // HEAD):
                ds = ds + full[i][h * HEAD:(h + 1) * HEAD]
            wc = wc_s[rows[i].stop - 1:rows[i].stop, sl[i]]
            state[i] = (state[i] + ds) * wc
            y_s[rows[i], sl[i]] = asrs[i][CHUNK:] + yb[i] + pre["kv"][i][CHUNK:]

    n_chunks = T // CHUNK
    state = [state_s[i] for i in cs]
    pre = {}
    for _ in state_free(0, pre):
        pass
    for c in range(n_chunks):
        nxt = {}
        streams = [state_step(c, pre, state)]
        if c + 1 < n_chunks:
            streams.append(state_free(c + 1, nxt))
        while streams:
            streams = [g for g in streams if next(g, streams) is not streams]
        pre = nxt
    for i in cs:
        state_s[i] = state[i]

    for b in range(B):
        rows = slice(b * T, (b + 1) * T)
        y = y_s[rows, :]
        mean = gsum(y) * (1.0 / HEAD)
        d = y - mean
        var = gsum(d * d) * (1.0 / HEAD)
        yn = d * lax.rsqrt(var + LNX_EPS) * lw_ref[...] + lb_ref[...]
        y_ref[b] = ((yn + bonus_s[rows, :]) * g_s[rows, :]).astype(y_ref.dtype)


def _rwkv(rw, mu, w0, w2p, a0, a2p, g2p, k_k, k_a, r_k, lnx_w, lnx_b, gmat, ltri, B, S, T):
    const = lambda i: (0, 0)
    vec = pl.BlockSpec((1, R_WIDTH), const)
    big = lambda dt: pltpu.VMEM((B * T, R_WIDTH), dt)
    y = pl.pallas_call(
        _rwkv_kernel,
        grid=(S // T,),
        in_specs=[
            pl.BlockSpec((B, T, R_COLS_PAD), lambda i: (0, i, 0)),
            pl.BlockSpec((1, R_COLS_PAD), const),
            vec,
            pl.BlockSpec((LANES, R_WIDTH), const),
            vec,
            pl.BlockSpec((LANES, R_WIDTH), const),
            pl.BlockSpec((GLAT_PAD, R_WIDTH), const),
            vec, vec, vec, vec, vec,
            pl.BlockSpec((R_WIDTH, R_WIDTH), const),
            pl.BlockSpec((T, T), const),
        ],
        out_specs=pl.BlockSpec((B, T, R_WIDTH), lambda i: (0, i, 0)),
        out_shape=jax.ShapeDtypeStruct((B, S, R_WIDTH), MXU_DTYPE),
        scratch_shapes=[
            pltpu.VMEM((B * (R_WIDTH // SLAB), HEAD, SLAB), F32),
            pltpu.VMEM((B, 8, R_COLS_PAD), F32),
            big(MXU_DTYPE), big(MXU_DTYPE), big(MXU_DTYPE), big(MXU_DTYPE),
            big(F32), big(F32), big(F32), big(F32), big(F32),
        ],
        compiler_params=pltpu.CompilerParams(
            dimension_semantics=("arbitrary",), vmem_limit_bytes=VMEM_LIMIT["rwkv"]),
        name="rwkv",
    )(rw.reshape(B, S, R_COLS_PAD), mu, w0, w2p, a0, a2p, g2p, k_k, k_a, r_k, lnx_w, lnx_b, gmat, ltri)
    return y.reshape(B * S, R_WIDTH)


def _cmp_kernel(x_ref, w1_ref, b1_ref, w2_ref, pe_ref, pos_ref, freq_ref, sgn_ref, o_ref, ot_ref):
    n_blk = x_ref.shape[0]
    half = x_ref.shape[1]
    j = pl.program_id(1)
    xb = _mx(x_ref[...])
    w1 = w1_ref[...]
    first_half = _dot(xb, w1[:half])
    second_half = pltpu.roll(_dot(xb, w1[half:]), n_blk - 1, 0)
    bias = _dot(_mx(pe_ref[...]), w1)[0:1] + b1_ref[...]
    hdn = first_half + second_half + bias
    hdn = hdn * jax.nn.sigmoid(hdn)
    out = _dot(_mx(hdn), w2_ref[...])
    cos, sin = _rope_tables(pos_ref[...], freq_ref[...], sgn_ref[...])
    lane = lax.broadcasted_iota(jnp.int32, (1, LANES), 1)
    roped = _rope128(out, cos, sin, (lane % HEAD) < ROT_HALF)
    res = jnp.where(j < N_KV_HEADS, roped, out)
    row = lax.broadcasted_iota(jnp.int32, (n_blk, 1), 0)
    res = jnp.where(row == n_blk - 1, 0.0, res)
    o_ref[...] = res[:, :HEAD].astype(o_ref.dtype)
    ot_ref[...] = res.T[:HEAD].astype(ot_ref.dtype)


def _compress(xblk, w1s, b1s, w2s, pes, pos_c, freq, sgn, B, n_blk):
    half = xblk.shape[-1]
    return pl.pallas_call(
        _cmp_kernel,
        grid=(B, 2 * N_KV_HEADS),
        in_specs=[
            pl.BlockSpec((None, None, n_blk, half), lambda b, j: (j, b, 0, 0)),
            pl.BlockSpec((None, 2 * half, CMP_HIDDEN), lambda b, j: (j // N_KV_HEADS, 0, 0)),
            pl.BlockSpec((None, 1, CMP_HIDDEN), lambda b, j: (j // N_KV_HEADS, 0, 0)),
            pl.BlockSpec((None, CMP_HIDDEN, LANES), lambda b, j: (j // N_KV_HEADS, 0, 0)),
            pl.BlockSpec((None, 8, 2 * half), lambda b, j: (j // N_KV_HEADS, 0, 0)),
            pl.BlockSpec((None, n_blk, 1), lambda b, j: (b, 0, 0)),
            pl.BlockSpec((1, LANES), lambda b, j: (0, 0)),
            pl.BlockSpec((1, LANES), lambda b, j: (0, 0)),
        ],
        out_specs=[
            pl.BlockSpec((None, None, n_blk, HEAD), lambda b, j: (b, j, 0, 0)),
            pl.BlockSpec((None, None, HEAD, n_blk), lambda b, j: (b, j, 0, 0)),
        ],
        out_shape=[
            jax.ShapeDtypeStruct((B, 2 * N_KV_HEADS, n_blk, HEAD), MXU_DTYPE),
            jax.ShapeDtypeStruct((B, 2 * N_KV_HEADS, HEAD, n_blk), MXU_DTYPE),
        ],
        compiler_params=pltpu.CompilerParams(
            dimension_semantics=("parallel", "parallel"), vmem_limit_bytes=VMEM_LIMIT["compress"]),
        name="compress",
    )(xblk, w1s, b1s, w2s, pes, pos_c, freq, sgn)


def _nsa_kernel(q_ref, kc_ref, vct_ref, ks_ref, vst_ref, kw_ref, vwt_ref, g_ref, o_ref,
                sel_s, ocmp_s, owin_s, sbuf_s, *, n_top, levels):
    n_blk = kc_ref.shape[0]
    n_sel = sel_s.shape[0]
    hk = pl.program_id(1)
    qb = pl.program_id(2)
    start = qb * Q_BLOCK
    W = GQA * Q_BLOCK

    q_t = q_ref[...].astype(F32).T
    qt = _mx(jnp.concatenate([q_t[g * HEAD:(g + 1) * HEAD] for g in range(GQA)], axis=1))
    tile4 = lambda t: jnp.concatenate([t] * GQA, axis=1)
    qi = lax.broadcasted_iota(jnp.int32, (1, Q_BLOCK), 1)
    t_q = start + qi
    t_w = tile4(t_q)

    tiles = GROUP // KV_TILE

    def compressed_and_select(n_c, n_s):
        w0 = pl.multiple_of(jnp.maximum(start - WINDOW, 0), LANES)
        n_win = WINDOW + Q_BLOCK
        s_win = _dot(kw_ref[pl.ds(w0, n_win), :], qt)

        c_end = lax.broadcasted_iota(jnp.int32, (n_c, 1), 0) * CMP_STRIDE + (CMP_BLOCK - 1)
        sc = _dot(kc_ref[0:n_c, :], qt) + tile4(jnp.where(c_end <= t_q, 0.0, MASK))
        q_plain = jnp.concatenate([qt, jnp.zeros((LANES - HEAD, W), MXU_DTYPE)], axis=0)
        s_first = [_dot(ks_ref[i * KV_TILE:(i + 1) * KV_TILE, :], q_plain) for i in range(tiles)]

        tok = w0 + lax.broadcasted_iota(jnp.int32, (n_win, 1), 0)
        in_win = (tok <= t_q) & (tok > t_q - WINDOW)
        s_win = s_win + tile4(jnp.where(in_win, 0.0, MASK))
        p_win = jnp.exp2(s_win - jnp.max(s_win, axis=0, keepdims=True))
        l_win = jnp.sum(p_win, axis=0, keepdims=True)
        p_win = _mx(p_win)
        acc_win = jnp.zeros((HEAD, W), F32)
        for i in range(n_win // LANES):
            acc_win = acc_win + _dot(vwt_ref[w0 // LANES + i], p_win[i * LANES:(i + 1) * LANES])
        owin_s[...] = acc_win * (1.0 / l_win)

        m = jnp.max(sc, axis=0, keepdims=True)
        p = jnp.exp2(sc - m)
        l = jnp.sum(p, axis=0, keepdims=True)
        pn = p * jnp.where(t_w >= CMP_BLOCK - 1, 1.0 / l, 0.0)
        ocmp_s[...] = _dot(vct_ref[:, 0:n_c], _mx(pn))

        ps = pn[:, 0:Q_BLOCK]
        for g in range(1, GQA):
            ps = ps + pn[:, g * Q_BLOCK:(g + 1) * Q_BLOCK]
        hi = _mx(ps)
        lo = _mx(ps - hi.astype(F32))
        j_col = lax.broadcasted_iota(jnp.int32, (n_s, 1), 0)
        c_row = lax.broadcasted_iota(jnp.int32, (1, n_c), 1)
        cover = ((c_row * CMP_STRIDE < j_col * SEL_BLOCK + SEL_BLOCK)
                 & (c_row * CMP_STRIDE + CMP_BLOCK > j_col * SEL_BLOCK))
        cover = jnp.where(cover, 1.0, 0.0).astype(MXU_DTYPE)
        imp = _dot(cover, hi) + _dot(cover, lo)

        cur = t_q // SEL_BLOCK
        forced = (j_col == 0) | (j_col == cur) | (j_col == cur - 1)
        score = jnp.where(forced, REMOVED, jnp.where(j_col <= cur, imp, MASK))
        for _ in range(max(n_top - 3, 0)):
            mx = jnp.max(score, axis=0, keepdims=True)
            idx = jnp.min(jnp.where(score == mx, j_col, n_s), axis=0, keepdims=True)
            score = jnp.where(j_col == idx, REMOVED, score)
        sel = jnp.where(score == REMOVED, 1.0, 0.0)
        sel_s[0:n_s, :] = sel

        neg = jnp.where(sel[0:SEL_GROUP] > 0.5, 0.0, MASK)
        per_tile = KV_TILE // SEL_BLOCK
        for i in range(tiles):
            bias = jnp.concatenate([jnp.broadcast_to(neg[r:r + 1], (SEL_BLOCK, Q_BLOCK))
                                    for r in range(i * per_tile, (i + 1) * per_tile)], axis=0)
            sbuf_s[0, i * KV_TILE:(i + 1) * KV_TILE, :] = s_first[i] + tile4(bias)

    lo_qb = 0
    for hi_qb in levels:
        n_c = min(hi_qb * (Q_BLOCK // CMP_STRIDE), n_blk)
        n_s = min(hi_qb * (Q_BLOCK // SEL_BLOCK), n_sel)
        pl.when((qb >= lo_qb) & (qb < hi_qb))(functools.partial(compressed_and_select, n_c, n_s))
        lo_qb = hi_qb
    o_cmp = ocmp_s[...]

    zero_rows = jnp.zeros((LANES - HEAD - 2 * SEL_GROUP, W), MXU_DTYPE)

    def group_scores(j):
        grp = sel_s[pl.ds(pl.multiple_of(j * SEL_GROUP, SEL_GROUP), SEL_GROUP), :]
        neg = jnp.where(grp > 0.5, 0.0, MASK)
        neg = tile4(jnp.concatenate([neg, jnp.zeros_like(neg)], axis=0))
        q_aug = jnp.concatenate([qt, _mx(neg), zero_rows], axis=0)
        r0 = pl.multiple_of(j * GROUP, GROUP)
        return [_dot(ks_ref[pl.ds(r0 + i * KV_TILE, KV_TILE), :], q_aug) for i in range(tiles)]

    def put_scores(slot, j):
        for i, s in enumerate(group_scores(j)):
            sbuf_s[slot, i * KV_TILE:(i + 1) * KV_TILE, :] = s

    def online(s, vt_tile, carry):
        m_, acc = carry
        m_new = jnp.maximum(m_, jnp.max(s, axis=0, keepdims=True))
        alpha = jnp.exp2(m_ - m_new)
        return m_new, alpha * acc + _dot(vt_tile, _mx(jnp.exp2(s - m_new)))

    def consume(slot, j, carry):
        for i in range(tiles):
            s = sbuf_s[slot, i * KV_TILE:(i + 1) * KV_TILE, :]
            carry = online(s, vst_ref[j * tiles + i], carry)
        return carry

    def pair_body(jp, carry):
        j = 2 * jp
        put_scores(1, j + 1)
        carry = consume(0, j, carry)
        put_scores(0, j + 2)
        return consume(1, j + 1, carry)

    def single_body(j, carry):
        carry = consume(0, j, carry)
        put_scores(0, j + 1)
        return carry

    init = (jnp.full((1, W), MASK, F32), jnp.zeros((HEAD + ONES_ROWS, W), F32))
    last = qb // (GROUP // Q_BLOCK)
    quad_body = lambda jq, c: pair_body(2 * jq + 1, pair_body(2 * jq, c))
    oct_body = lambda jo, c: quad_body(2 * jo + 1, quad_body(2 * jo, c))
    carry = lax.fori_loop(0, last // 8, oct_body, init)
    carry = lax.fori_loop(2 * (last // 8), last // 4, quad_body, carry)
    carry = lax.fori_loop(2 * (last // 4), last // 2, pair_body, carry)
    carry = lax.fori_loop(2 * (last // 2), last, single_body, carry)

    k_row = lax.broadcasted_iota(jnp.int32, (KV_TILE, 1), 0)
    for i in range(tiles):
        tok = last * GROUP + i * KV_TILE + k_row
        causal = tile4(jnp.where(tok <= t_q, 0.0, MASK))
        s = sbuf_s[0, i * KV_TILE:(i + 1) * KV_TILE, :] + causal
        carry = online(s, vst_ref[last * tiles + i], carry)
    _, acc_sel = carry
    o_sel = acc_sel[:HEAD] * (1.0 / acc_sel[HEAD:HEAD + 1])
    o_win = owin_s[...]

    outs = []
    gates = g_ref[...]
    per_kv = GQA * 3
    gate = lambda r: jnp.where(hk == 0, gates[r:r + 1], gates[per_kv + r:per_kv + r + 1])
    for g in range(GQA):
        cols = slice(g * Q_BLOCK, (g + 1) * Q_BLOCK)
        o_g = (gate(g * 3) * o_cmp[:, cols] + gate(g * 3 + 1) * o_sel[:, cols]
               + gate(g * 3 + 2) * o_win[:, cols])
        outs.append(o_g)
    o_ref[...] = jnp.concatenate(outs, axis=0).T.astype(o_ref.dtype)


def _nsa(q, kc, vct, ks, vst, kw, vwt, gt, B, S):
    BS = q.shape[0]
    nqb = S // Q_BLOCK
    n_blk = S // CMP_STRIDE
    n_sel = S // SEL_BLOCK
    n_top = min(SEL_TOPK, n_sel)
    step = max(4, nqb // CMP_VARIANTS // 4 * 4)
    levels = tuple(range(step, nqb, step)) + (nqb,)
    per_head = lambda *blk: pl.BlockSpec((None, None) + blk, lambda b, h, i: (b, h) + (0,) * len(blk))
    qspec = pl.BlockSpec((Q_BLOCK, GQA * HEAD), lambda b, h, i: (b * nqb + i, h))
    return pl.pallas_call(
        functools.partial(_nsa_kernel, n_top=n_top, levels=levels),
        grid=(B, N_KV_HEADS, nqb),
        in_specs=[
            qspec,
            per_head(n_blk, HEAD),
            per_head(HEAD, n_blk),
            per_head(S, LANES),
            per_head(S // KV_TILE, HEAD + ONES_ROWS, KV_TILE),
            per_head(S, HEAD),
            per_head(S // LANES, HEAD, LANES),
            pl.BlockSpec((None, GATE_ROWS, Q_BLOCK), lambda b, h, i: (b, 0, i)),
        ],
        out_specs=qspec,
        out_shape=jax.ShapeDtypeStruct((BS, N_WIDTH), MXU_DTYPE),
        scratch_shapes=[
            pltpu.VMEM((n_sel, Q_BLOCK), F32),
            pltpu.VMEM((HEAD, GQA * Q_BLOCK), F32),
            pltpu.VMEM((HEAD, GQA * Q_BLOCK), F32),
            pltpu.VMEM((2, GROUP, GQA * Q_BLOCK), F32),
        ],
        compiler_params=pltpu.CompilerParams(
            dimension_semantics=("parallel", "parallel", "arbitrary"), vmem_limit_bytes=VMEM_LIMIT["nsa"]),
        name="nsa",
    )(q, kc, vct, ks, vst, kw, vwt, gt)


def _ffn_kernel(x_ref, yr_ref, yn_ref, wo_ref, nf_ref, wg_ref, wu_ref, wd_ref, nfin_ref, o_ref,
                h1_s, xn_s, acc_s):
    f = pl.program_id(1)

    @pl.when(f == 0)
    def _():
        mix = _dot(yr_ref[...], wo_ref[:R_WIDTH]) + _dot(yn_ref[...], wo_ref[R_WIDTH:])
        h1 = x_ref[...] + mix
        h1_s[...] = h1
        xn_s[...] = _rms(h1, nf_ref[...]).astype(xn_s.dtype)
        acc_s[...] = jnp.zeros_like(acc_s)

    xn = xn_s[...]
    gate = _dot(xn, wg_ref[...])
    up = _dot(xn, wu_ref[...])
    acc_s[...] += _dot(_mx(gate * jax.nn.sigmoid(gate) * up), wd_ref[...])

    @pl.when(f == pl.num_programs(1) - 1)
    def _():
        o_ref[...] = _rms(h1_s[...] + acc_s[...], nfin_ref[...])


def _ffn(x2, y_r, y_n, w_out, nf, w_gate, w_up, w_down, nfin, tm, tf):
    BS, D = x2.shape
    d_ff = w_gate.shape[1]
    row = lambda i, f: (i, 0)
    const = lambda i, f: (0, 0)
    return pl.pallas_call(
        _ffn_kernel,
        grid=(BS // tm, d_ff // tf),
        in_specs=[
            pl.BlockSpec((tm, D), row),
            pl.BlockSpec((tm, R_WIDTH), row),
            pl.BlockSpec((tm, N_WIDTH), row),
            pl.BlockSpec((R_WIDTH + N_WIDTH, D), const),
            pl.BlockSpec((1, D), const),
            pl.BlockSpec((D, tf), lambda i, f: (0, f)),
            pl.BlockSpec((D, tf), lambda i, f: (0, f)),
            pl.BlockSpec((tf, D), lambda i, f: (f, 0)),
            pl.BlockSpec((1, D), const),
        ],
        out_specs=pl.BlockSpec((tm, D), row),
        out_shape=jax.ShapeDtypeStruct((BS, D), F32),
        scratch_shapes=[pltpu.VMEM((tm, D), F32), pltpu.VMEM((tm, D), MXU_DTYPE), pltpu.VMEM((tm, D), F32)],
        compiler_params=pltpu.CompilerParams(
            dimension_semantics=("parallel", "arbitrary"), vmem_limit_bytes=VMEM_LIMIT["ffn"]),
        name="ffn",
    )(x2, y_r, y_n, w_out, nf, w_gate, w_up, w_down, nfin)


def _pad_to(a, n, axis):
    pad = [(0, 0)] * a.ndim
    pad[axis] = (0, n - a.shape[axis])
    return jnp.pad(a, pad)


def _rope_rows():
    lane = jnp.arange(LANES) % HEAD
    inv_freq = jnp.power(jnp.float32(ROPE_THETA), -jnp.arange(ROT_HALF, dtype=F32) * 2.0 / ROT_DIM)
    freq = jnp.where(lane < ROT_DIM, inv_freq[lane % ROT_HALF], 0.0).astype(F32)[None, :]
    sgn = jnp.where(lane < ROT_HALF, -1.0, jnp.where(lane < ROT_DIM, 1.0, 0.0)).astype(F32)[None, :]
    return freq, sgn


def _layer(h, positions, norm_mix, w_in, mu_shift, w0, w2, a0, a2, g2, k_k, k_a, r_k, lnx_w, lnx_b,
           pe_k, wk1, bk1, wk2, pe_v, wv1, bv1, wv2, w_out, norm_ffn, w_gate, w_up, w_down, norm_final):
    B, S, D = h.shape
    BS = B * S
    x2 = h.reshape(BS, D)
    freq, sgn = _rope_rows()
    row = lambda a: a.reshape(1, -1).astype(F32)

    w_r = _mx(_pad_to(w_in[:, :R_COLS], R_COLS_PAD, 1))
    w_n = _mx(_pad_to(w_in[:, R_COLS:], N_COLS_PAD, 1))
    tiles = _tiles(S, BS, w_gate.shape[1])
    rw, q, kvc, ks, vst, kw, vwt, gt = _proj(
        x2, positions.reshape(BS, 1), row(norm_mix), freq, sgn, w_r, w_n, B, S, tiles["proj_rows"])

    T = tiles["rwkv_rows"]
    head_of = jnp.arange(R_WIDTH) // HEAD
    gmat = (head_of[:, None] == head_of[None, :]).astype(MXU_DTYPE)
    tt = jnp.arange(T)
    ltri = ((tt[:, None] // CHUNK == tt[None, :] // CHUNK) & (tt[:, None] >= tt[None, :])).astype(MXU_DTYPE)
    w2p = _mx(_pad_to(w2, LANES, 0))
    a2p = _mx(jnp.concatenate([jnp.zeros((DECAY_LORA, R_WIDTH), F32), a2], axis=0))
    g2p = _mx(_pad_to(g2, GLAT_PAD, 0))
    y_r = _rwkv(rw, row(_pad_to(mu_shift, R_COLS_PAD, 0)), row(w0), w2p, row(a0), a2p, g2p,
                row(k_k), row(k_a), row(r_k), row(lnx_w), row(lnx_b), gmat, ltri, B, S, T)

    n_blk = S // CMP_STRIDE
    xblk = kvc.reshape(2 * N_KV_HEADS, B, n_blk, CMP_STRIDE * HEAD)
    w1s = _mx(jnp.stack([wk1, wv1]))
    b1s = jnp.stack([bk1, bv1]).reshape(2, 1, CMP_HIDDEN).astype(F32)
    w2s = _mx(_pad_to(jnp.stack([wk2, wv2]), LANES, 2))
    pes = _pad_to(jnp.stack([pe_k, pe_v]).reshape(2, 1, CMP_BLOCK * HEAD), 8, 1).astype(F32)
    pos_c = _pad_to(positions[:, CMP_BLOCK - 1::CMP_STRIDE], n_blk, 1).reshape(B, n_blk, 1)
    cmp_rows, cmp_t = _compress(xblk, w1s, b1s, w2s, pes, pos_c, freq, sgn, B, n_blk)
    kc = cmp_rows[:, :N_KV_HEADS]
    vct = cmp_t[:, N_KV_HEADS:]

    y_n = _nsa(q, kc, vct, ks, vst, kw, vwt, gt, B, S)

    out = _ffn(x2, y_r, y_n, _mx(w_out), row(norm_ffn), _mx(w_gate), _mx(w_up), _mx(w_down),
               row(norm_final), tiles["ffn_rows"], tiles["ffn_cols"])
    return out.reshape(B, S, D)


def kernel(x, positions, norm_mix, w_in, mu_shift, w0, w2, a0, a2, g2, k_k, k_a, r_k, lnx_w, lnx_b, pe_k, wk1, bk1, wk2, pe_v, wv1, bv1, wv2, w_out, norm_ffn, w_gate, w_up, w_down, norm_final):
    assert norm_mix.shape[0] == 1, "single-layer block"
    l = 0
    return _layer(x, positions, norm_mix[l], w_in[l], mu_shift[l], w0[l], w2[l], a0[l], a2[l], g2[l],
                  k_k[l], k_a[l], r_k[l].reshape(-1), lnx_w[l], lnx_b[l], pe_k[l], wk1[l], bk1[l], wk2[l],
                  pe_v[l], wv1[l], bv1[l], wv2[l], w_out[l], norm_ffn[l], w_gate[l], w_up[l], w_down[l],
                  norm_final)
```
